```python
import math
import jax, jax.numpy as jnp
from jax import lax
import numpy as np

D_MODEL = 2048
BATCH = 2
SEQ = 4096
DEPTH = 2
DEC_BATCH = 128
DEC_SEQ = 4
PAST_LEN = 2048
PAGE_SIZE = 128

PLE_DIM = 256
D_FF = 4 * D_MODEL
MIX_WIDTH = D_MODEL
GLA_HEADS = 4
GLA_DV = MIX_WIDTH // 2 // GLA_HEADS
GLA_DK = GLA_DV // 2
GLA_GATE_RANK = 16
GLA_TAU = 16.0
GLA_CHUNK = 64
ATT_HEADS = 8
ATT_DH = MIX_WIDTH // 2 // ATT_HEADS
DILATED_BRANCHES = ((128, 1), (512, 4), (2048, 16))
MAX_WINDOW = 2048
EPS = 1e-6
NEG_INF = -1e30

GLA_QK = GLA_HEADS * GLA_DK
GLA_V = GLA_HEADS * GLA_DV
ATT_W = ATT_HEADS * ATT_DH
IN_SIZES = (GLA_QK, GLA_QK, GLA_V, GLA_V, GLA_GATE_RANK, ATT_W, ATT_W, ATT_W)
IN_COLS = sum(IN_SIZES)
SPLIT_POINTS = tuple(int(s) for s in np.cumsum(IN_SIZES)[:-1])

kernel_name = "hybrid_gla_dilated_decoder_step"


def rmsnorm(x, g):
    xf = x.astype(jnp.float32)
    y = xf * lax.rsqrt(jnp.mean(xf * xf, axis=-1, keepdims=True) + EPS)
    return (y * g.astype(jnp.float32)).astype(x.dtype)


def alibi_slopes():
    return jnp.asarray(2.0 ** (-8.0 * np.arange(1, ATT_HEADS + 1) / ATT_HEADS), dtype=jnp.float32)


def gla_chunked(q, k, v, log_a, s0):
    B, T, H, DK = q.shape
    DV = v.shape[-1]
    C = math.gcd(T, GLA_CHUNK)
    N = T // C
    f32 = jnp.float32
    rs = lambda a: a.astype(f32).reshape(B, N, C, H, a.shape[-1])
    q, k, v, g = rs(q), rs(k), rs(v), rs(log_a)
    b = jnp.cumsum(g, axis=2)
    b_last = b[:, :, -1:]
    q_dec = q * jnp.exp(b)
    k_dec = k * jnp.exp(-b)
    k_end = k * jnp.exp(b_last - b)
    causal = jnp.tril(jnp.ones((C, C), dtype=bool))
    A = jnp.einsum('bnthk,bnshk->bnhts', q_dec, k_dec)
    A = jnp.where(causal, A, 0.0)
    o_intra = jnp.einsum('bnhts,bnshv->bnthv', A, v)
    dS = jnp.einsum('bnchk,bnchv->bnhkv', k_end, v)
    decay = jnp.exp(b_last[:, :, 0])

    def step(S, inp):
        dec, ds = inp
        return dec[..., None] * S + ds, S

    S_final, S_starts = lax.scan(step, s0.astype(f32), (jnp.moveaxis(decay, 1, 0), jnp.moveaxis(dS, 1, 0)))
    S_starts = jnp.moveaxis(S_starts, 0, 1)
    o_inter = jnp.einsum('bnthk,bnhkv->bnthv', q_dec, S_starts)
    o = (o_intra + o_inter).reshape(B, T, H, DV)
    return o, S_final


def _fold(a, d, n, Tp):
    B, T, H, E = a.shape
    Lp = Tp // d
    a = jnp.pad(a, ((0, 0), (0, Tp - T), (0, 0), (0, 0)))
    a = a.reshape(B, Lp, d, H, E).transpose(0, 2, 1, 3, 4)
    return a.reshape(B, d, Lp // n, n, H, E)


def _unfold(a, T):
    B, d, nb, n, H, E = a.shape
    a = a.reshape(B, d, nb * n, H, E).transpose(0, 2, 1, 3, 4)
    return a.reshape(B, nb * n * d, H, E)[:, :T]


def _with_prev(a):
    prev = jnp.pad(a[:, :, :-1], ((0, 0), (0, 0), (1, 0), (0, 0), (0, 0), (0, 0)))
    return jnp.concatenate([prev, a], axis=3)


def _merge_branches(outs, lses):
    wts = jax.nn.softmax(jnp.stack(lses, 0), axis=0)
    return jnp.sum(wts[..., None] * jnp.stack(outs, 0), axis=0)


def dilated_prompt(q, k, v):
    B, T, H, E = q.shape
    slopes = alibi_slopes()
    scale = ATT_DH ** -0.5
    outs, lses = [], []
    for (w, d) in DILATED_BRANCHES:
        n = w // d
        L = -(-T // d)
        Lp = -(-L // n) * n
        Tp = Lp * d
        nb = Lp // n
        qb = _fold(q, d, n, Tp)
        kk = _with_prev(_fold(k, d, n, Tp))
        vv = _with_prev(_fold(v, d, n, Tp)).astype(jnp.float32)
        s = jnp.einsum('bdiqhe,bdikhe->bdihqk', qb, kk, preferred_element_type=jnp.float32) * scale
        qi = jnp.arange(n)[:, None]
        ki = jnp.arange(2 * n)[None, :]
        delta = n + qi - ki
        band = (delta >= 0) & (delta <= n)
        valid = band[None] & ((jnp.arange(nb)[:, None, None] > 0) | (ki >= n)[None])
        bias = -slopes[:, None, None] * (delta * d).astype(jnp.float32)[None]
        s = jnp.where(valid[:, None], s + bias, NEG_INF)
        m = jnp.max(s, axis=-1, keepdims=True)
        p = jnp.exp(s - m)
        l = jnp.sum(p, axis=-1, keepdims=True)
        o = jnp.einsum('bdihqk,bdikhe->bdiqhe', p, vv) / jnp.moveaxis(l, 3, 4)
        lse = jnp.moveaxis(m + jnp.log(l), 3, 4)
        outs.append(_unfold(o, T))
        lses.append(_unfold(lse, T)[..., 0])
    return _merge_branches(outs, lses).astype(q.dtype)


def dilated_sample(q, k, v, k_buf, v_buf):
    B, T, H, E = q.shape
    W = k_buf.shape[1]
    slopes = alibi_slopes()
    scale = ATT_DH ** -0.5
    kall = jnp.concatenate([k_buf.astype(k.dtype), k], axis=1)
    vall = jnp.concatenate([v_buf.astype(v.dtype), v], axis=1)
    outs, lses = [], []
    for (w, d) in DILATED_BRANCHES:
        n = w // d
        j = jnp.arange(n + 1)
        idx = W + jnp.arange(T)[:, None] - j[None, :] * d
        valid = idx >= 0
        idxc = jnp.maximum(idx, 0)
        kg = kall[:, idxc]
        vg = vall[:, idxc].astype(jnp.float32)
        s = jnp.einsum('bqhe,bqjhe->bqhj', q, kg, preferred_element_type=jnp.float32) * scale
        s = s - slopes[:, None] * (j * d).astype(jnp.float32)[None, :]
        s = jnp.where(valid[:, None, :], s, NEG_INF)
        m = jnp.max(s, axis=-1, keepdims=True)
        p = jnp.exp(s - m)
        l = jnp.sum(p, axis=-1, keepdims=True)
        o = jnp.einsum('bqhj,bqjhe->bqhe', p, vg) / l
        outs.append(o)
        lses.append((m + jnp.log(l))[..., 0])
    return _merge_branches(outs, lses).astype(q.dtype)


def trunk_layer(x, p_l, gla_s0, k_buf, v_buf, norm_mix, w_in, w_gate2, b_gate, gla_norm, w_out,
                norm_mlp, w_up, w_down, norm_ple, w_ple_gate, w_ple_proj):
    B, T, _ = x.shape
    hn = rmsnorm(x, norm_mix)
    z = hn @ w_in
    q_g, k_g, v_g, r_g, g_lr, q_a, k_a, v_a = jnp.split(z, SPLIT_POINTS, axis=-1)
    q_g = q_g.reshape(B, T, GLA_HEADS, GLA_DK) * (GLA_DK ** -0.5)
    k_g = k_g.reshape(B, T, GLA_HEADS, GLA_DK)
    v_g = v_g.reshape(B, T, GLA_HEADS, GLA_DV)
    log_a = jax.nn.log_sigmoid((g_lr @ w_gate2 + b_gate).astype(jnp.float32)) / GLA_TAU
    log_a = log_a.reshape(B, T, GLA_HEADS, GLA_DK)
    o_g, s_new = gla_chunked(q_g, k_g, v_g, log_a, gla_s0)
    o_g = rmsnorm(o_g.astype(x.dtype), gla_norm).reshape(B, T, GLA_V) * jax.nn.silu(r_g)
    q_a = q_a.reshape(B, T, ATT_HEADS, ATT_DH)
    k_a = k_a.reshape(B, T, ATT_HEADS, ATT_DH)
    v_a = v_a.reshape(B, T, ATT_HEADS, ATT_DH)
    if k_buf is None:
        o_a = dilated_prompt(q_a, k_a, v_a)
    else:
        o_a = dilated_sample(q_a, k_a, v_a, k_buf, v_buf)
    mix = jnp.concatenate([o_g, o_a.reshape(B, T, ATT_W)], axis=-1) @ w_out
    x = x + mix
    u = rmsnorm(x, norm_mlp) @ w_up
    x = x + jnp.square(jax.nn.relu(u)) @ w_down
    gate = jax.nn.sigmoid(rmsnorm(x, norm_ple) @ w_ple_gate)
    x = x + gate * (p_l @ w_ple_proj)
    return x, s_new.astype(x.dtype), k_a, v_a


def setup_inputs(seed: int = 0) -> dict:
    key = jax.random.key(seed)
    ks = jax.random.split(key, 24)
    f32 = jnp.float32
    w_buf = min(MAX_WINDOW, PAST_LEN)
    nrm = lambda k, shape, s: jax.random.normal(k, shape, f32) * s
    return {
        "x_prompt": nrm(ks[0], (BATCH, SEQ, D_MODEL), 1.0),
        "x_sample": nrm(ks[1], (DEC_BATCH, DEC_SEQ, D_MODEL), 1.0),
        "cache_k_win": nrm(ks[2], (DEPTH, DEC_BATCH, w_buf, ATT_HEADS, ATT_DH), 1.0),
        "cache_v_win": nrm(ks[3], (DEPTH, DEC_BATCH, w_buf, ATT_HEADS, ATT_DH), 1.0),
        "state_gla": nrm(ks[4], (DEPTH, DEC_BATCH, GLA_HEADS, GLA_DK, GLA_DV), 0.5),
        "p_prompt": nrm(ks[5], (DEPTH, BATCH, SEQ, PLE_DIM), 1.0),
        "p_sample": nrm(ks[6], (DEPTH, DEC_BATCH, DEC_SEQ, PLE_DIM), 1.0),
        "norm_mix": 1.0 + nrm(ks[7], (DEPTH, D_MODEL), 0.02),
        "w_in": nrm(ks[8], (DEPTH, D_MODEL, IN_COLS), D_MODEL ** -0.5),
        "w_gate2": nrm(ks[9], (DEPTH, GLA_GATE_RANK, GLA_QK), GLA_GATE_RANK ** -0.5),
        "b_gate": nrm(ks[10], (DEPTH, GLA_QK), 0.01),
        "gla_norm": 1.0 + nrm(ks[11], (DEPTH, GLA_DV), 0.02),
        "w_out": nrm(ks[12], (DEPTH, MIX_WIDTH, D_MODEL), MIX_WIDTH ** -0.5),
        "norm_mlp": 1.0 + nrm(ks[13], (DEPTH, D_MODEL), 0.02),
        "w_up": nrm(ks[14], (DEPTH, D_MODEL, D_FF), D_MODEL ** -0.5),
        "w_down": nrm(ks[15], (DEPTH, D_FF, D_MODEL), D_FF ** -0.5),
        "norm_ple": 1.0 + nrm(ks[16], (DEPTH, D_MODEL), 0.02),
        "w_ple_gate": nrm(ks[17], (DEPTH, D_MODEL, D_MODEL), D_MODEL ** -0.5),
        "w_ple_proj": nrm(ks[18], (DEPTH, PLE_DIM, D_MODEL), PLE_DIM ** -0.5),
        "norm_final": 1.0 + nrm(ks[19], (D_MODEL,), 0.02),
    }


def reference(x_prompt, x_sample, cache_k_win, cache_v_win, state_gla, p_prompt, p_sample,
              norm_mix, w_in, w_gate2, b_gate, gla_norm, w_out, norm_mlp, w_up, w_down,
              norm_ple, w_ple_gate, w_ple_proj, norm_final):
    hp, hs = x_prompt, x_sample
    bp, tp = x_prompt.shape[0], x_prompt.shape[1]
    w_keep = min(MAX_WINDOW, tp)
    kp_l, vp_l, gp_l, ks_l, vs_l, gs_l = [], [], [], [], [], []
    for l in range(DEPTH):
        wts = (norm_mix[l], w_in[l], w_gate2[l], b_gate[l], gla_norm[l], w_out[l],
               norm_mlp[l], w_up[l], w_down[l], norm_ple[l], w_ple_gate[l], w_ple_proj[l])
        s0 = jnp.zeros((bp, GLA_HEADS, GLA_DK, GLA_DV), jnp.float32)
        hp, sp, kp, vp = trunk_layer(hp, p_prompt[l], s0, None, None, *wts)
        hs, ss, kss, vss = trunk_layer(hs, p_sample[l], state_gla[l], cache_k_win[l], cache_v_win[l], *wts)
        kp_l.append(kp[:, tp - w_keep:])
        vp_l.append(vp[:, tp - w_keep:])
        gp_l.append(sp)
        ks_l.append(kss)
        vs_l.append(vss)
        gs_l.append(ss)
    y_prompt = rmsnorm(hp, norm_final)
    y_sample = rmsnorm(hs, norm_final)
    return (y_prompt, y_sample, jnp.stack(kp_l), jnp.stack(vp_l), jnp.stack(gp_l),
            jnp.stack(ks_l), jnp.stack(vs_l), jnp.stack(gs_l))
```

```python
import functools

import jax
import jax.numpy as jnp
import numpy as np
from jax import lax
from jax.experimental import pallas as pl
from jax.experimental.pallas import tpu as pltpu

F32 = jnp.float32
BF16 = jnp.bfloat16

EPS = 1e-6
NEG_INF = -1e30
GLA_HEADS = 4
GLA_DK = 128
GLA_DV = 256
GLA_TAU = 16.0
GLA_CHUNK = 64
ATT_HEADS = 8
ATT_DH = 128
DILATED_BRANCHES = ((128, 1), (512, 4), (2048, 16))
MAX_WINDOW = 2048
GATE_RANK = 16

LANES = 128
VMEM_LIMIT = 56 * 1024 * 1024

Z_COLS = 6144
GLA_PART = 3072
ATT_PART = 3072
ZC_QG, ZC_KG, ZC_VG, ZC_RG, ZC_QA, ZC_KA, ZC_VA = 0, 512, 1024, 2048, 3072, 4096, 5120


def _params(sem, vmem=VMEM_LIMIT):
    return pltpu.CompilerParams(dimension_semantics=sem, vmem_limit_bytes=vmem)


def _rms(x, g):
    return x * lax.rsqrt(jnp.mean(x * x, axis=-1, keepdims=True) + EPS) * g


def _dot(a, b):
    return jnp.dot(a.astype(BF16), b.astype(BF16), preferred_element_type=F32)


def _dot_nt(a, b):
    return lax.dot_general(a.astype(BF16), b.astype(BF16), (((1,), (1,)), ((), ())),
                           preferred_element_type=F32)


def _split2(x):
    hi = x.astype(BF16)
    lo = (x - hi.astype(F32)).astype(BF16)
    return hi, lo


def _dot_hi(a, b):
    ah, al = _split2(a)
    bh, bl = _split2(b)
    d = lambda u, v: jnp.dot(u, v, preferred_element_type=F32)
    return d(ah, bh) + d(ah, bl) + d(al, bh)


def _dot_exact(m, x):
    hi = x.astype(BF16)
    r1 = x - hi.astype(F32)
    mid = r1.astype(BF16)
    lo = (r1 - mid.astype(F32)).astype(BF16)
    d = lambda v: jnp.dot(m, v, preferred_element_type=F32)
    return d(hi) + d(mid) + d(lo)


def _log_sigmoid(x):
    return jnp.minimum(x, 0.0) - jnp.log1p(jnp.exp(-jnp.abs(x)))


def _silu(x):
    return x * jax.nn.sigmoid(x)


def _inproj_kernel(x_ref, g_ref, wa_ref, wb_ref, wl_ref, z_ref, glr_ref, xn_ref, *, nja):
    j = pl.program_id(1)

    @pl.when(j == 0)
    def _():
        xb = _rms(x_ref[...], g_ref[...]).astype(BF16)
        xn_ref[...] = xb
        glr_ref[...] = jnp.dot(xb, wl_ref[...].astype(BF16), preferred_element_type=F32)

    @pl.when(j < nja)
    def _():
        z_ref[...] = jnp.dot(xn_ref[...], wa_ref[...].astype(BF16), preferred_element_type=F32)

    @pl.when(j >= nja)
    def _():
        z_ref[...] = jnp.dot(xn_ref[...], wb_ref[...].astype(BF16), preferred_element_type=F32)


def _inproj(x, g, w_in, w_att, w_lr, l, tm, tn=512):
    m, d = x.shape
    nja = GLA_PART // tn
    nj = Z_COLS // tn
    return pl.pallas_call(
        functools.partial(_inproj_kernel, nja=nja),
        grid=(m // tm, nj),
        in_specs=[
            pl.BlockSpec((tm, d), lambda i, j: (i, 0)),
            pl.BlockSpec((None, 1, d), lambda i, j: (l, 0, 0)),
            pl.BlockSpec((None, d, tn), lambda i, j: (l, 0, jnp.minimum(j, nja - 1))),
            pl.BlockSpec((None, d, tn), lambda i, j: (l, 0, jnp.maximum(j - nja, 0))),
            pl.BlockSpec((None, d, LANES), lambda i, j: (l, 0, 0)),
        ],
        out_specs=[
            pl.BlockSpec((tm, tn), lambda i, j: (i, j)),
            pl.BlockSpec((tm, LANES), lambda i, j: (i, 0)),
        ],
        out_shape=[jax.ShapeDtypeStruct((m, Z_COLS), F32),
                   jax.ShapeDtypeStruct((m, LANES), F32)],
        scratch_shapes=[pltpu.VMEM((tm, d), BF16)],
        compiler_params=_params(("arbitrary", "arbitrary")),
        name="inproj",
    )(x, g, w_in, w_att, w_lr)


def _gla_prompt_kernel(q_ref, k_ref, v_ref, r_ref, glr_ref, w2_ref, bg_ref, gn_ref,
                       og_ref, sout_ref, s_ref, *, tb, chunk):
    step = pl.program_id(1)

    @pl.when(step == 0)
    def _():
        s_ref[...] = jnp.zeros_like(s_ref)

    c = chunk
    ri = lax.broadcasted_iota(jnp.int32, (c, c), 0)
    ci = lax.broadcasted_iota(jnp.int32, (c, c), 1)
    causal = ci <= ri
    tri = causal.astype(BF16)
    w2 = w2_ref[...]
    bg = bg_ref[...]
    gn = gn_ref[...]
    zpad_k = jnp.zeros((LANES - c, GLA_DK), F32)
    zpad_v = jnp.zeros((LANES - c, GLA_DV), BF16)

    def body(n, carry):
        r0 = pl.multiple_of(n * c, c)
        rows = pl.ds(r0, c)
        x = _dot_hi(glr_ref[rows, :], w2) + bg
        g = _log_sigmoid(x) * (1.0 / GLA_TAU)
        b = _dot_exact(tri, g)
        for h in range(GLA_HEADS):
            ks = slice(h * GLA_DK, (h + 1) * GLA_DK)
            vs = slice(h * GLA_DV, (h + 1) * GLA_DV)
            bh = b[:, ks]
            blast = bh[c - 1:c, :]
            q = q_ref[rows, ks] * (GLA_DK ** -0.5)
            k = k_ref[rows, ks]
            v = v_ref[rows, vs].astype(BF16)
            qd = (q * jnp.exp(bh)).astype(BF16)
            kd = k * jnp.exp(-bh)
            ke = k * jnp.exp(blast - bh)
            a = jnp.where(causal, _dot_nt(qd, kd), 0.0)
            s_old = s_ref[h]
            o = _dot(a, v) + _dot(qd, s_old)
            ke_t = jnp.concatenate([ke, zpad_k], axis=0).T.astype(BF16)
            v_pad = jnp.concatenate([v, zpad_v], axis=0)
            ds = jnp.dot(ke_t, v_pad, preferred_element_type=F32)
            dec = jnp.broadcast_to(jnp.exp(blast), (GLA_DK, GLA_DK)).T
            s_ref[h] = jnp.concatenate([dec, dec], axis=1) * s_old + ds
            on = _rms(o, gn)
            og_ref[rows, vs] = on * _silu(r_ref[rows, vs])
        return carry

    lax.fori_loop(0, tb // c, body, 0)

    @pl.when(step == pl.num_programs(1) - 1)
    def _():
        sout_ref[...] = s_ref[...]


def _gla_prompt(z3, glr3, w2p, bg, gn, l, tb=512):
    b, t, _ = z3.shape
    chunk = np.gcd(t, GLA_CHUNK)
    assert t % tb == 0 and tb % chunk == 0
    qk_w = GLA_HEADS * GLA_DK
    v_w = GLA_HEADS * GLA_DV
    return pl.pallas_call(
        functools.partial(_gla_prompt_kernel, tb=tb, chunk=int(chunk)),
        grid=(b, t // tb),
        in_specs=[
            pl.BlockSpec((None, tb, qk_w), lambda i, s: (i, s, ZC_QG // qk_w)),
            pl.BlockSpec((None, tb, qk_w), lambda i, s: (i, s, ZC_KG // qk_w)),
            pl.BlockSpec((None, tb, v_w), lambda i, s: (i, s, ZC_VG // v_w)),
            pl.BlockSpec((None, tb, v_w), lambda i, s: (i, s, ZC_RG // v_w)),
            pl.BlockSpec((None, tb, LANES), lambda i, s: (i, s, 0)),
            pl.BlockSpec((None, LANES, qk_w), lambda i, s: (l, 0, 0)),
            pl.BlockSpec((None, 1, qk_w), lambda i, s: (l, 0, 0)),
            pl.BlockSpec((None, 1, GLA_DV), lambda i, s: (l, 0, 0)),
        ],
        out_specs=[
            pl.BlockSpec((None, tb, v_w), lambda i, s: (i, s, 0)),
            pl.BlockSpec((None, GLA_HEADS, GLA_DK, GLA_DV), lambda i, s: (i, 0, 0, 0)),
        ],
        out_shape=[jax.ShapeDtypeStruct((b, t, v_w), F32),
                   jax.ShapeDtypeStruct((b, GLA_HEADS, GLA_DK, GLA_DV), F32)],
        scratch_shapes=[pltpu.VMEM((GLA_HEADS, GLA_DK, GLA_DV), F32)],
        compiler_params=_params(("arbitrary", "arbitrary")),
        name="gla_prompt",
    )(z3, z3, z3, z3, glr3, w2p, bg, gn)


def _gla_sample_kernel(q_ref, k_ref, v_ref, r_ref, glr_ref, w2_ref, bg_ref, gn_ref, s0_ref,
                       og_ref, sn_ref, *, nb, tq):
    rws = nb * tq
    sh = int(np.log2(tq))
    ri = lax.broadcasted_iota(jnp.int32, (rws, rws), 0)
    ci = lax.broadcasted_iota(jnp.int32, (rws, rws), 1)
    same = (ri >> sh) == (ci >> sh)
    causal = same & (ci <= ri)
    x = _dot_hi(glr_ref[...], w2_ref[...]) + bg_ref[...]
    g = _log_sigmoid(x) * (1.0 / GLA_TAU)
    b = _dot_exact(causal.astype(BF16), g)
    bl = _dot_exact(same.astype(BF16), g)
    q = q_ref[...] * (GLA_DK ** -0.5)
    k = k_ref[...]
    qd = (q * jnp.exp(b)).astype(BF16)
    kd = k * jnp.exp(-b)
    ke = k * jnp.exp(bl - b)
    v = v_ref[...].astype(BF16)
    a = jnp.where(causal, _dot_nt(qd, kd), 0.0)
    o = _dot(a, v)
    ke_t = ke.T
    dec_t = jnp.exp(bl).T
    row_seq = lax.broadcasted_iota(jnp.int32, (rws, GLA_DV), 0) >> sh
    col_seq = lax.broadcasted_iota(jnp.int32, (GLA_DK, rws), 1) >> sh
    for s in range(nb):
        s0 = s0_ref[s]
        o = o + jnp.where(row_seq == s, _dot(qd, s0), 0.0)
        lhs = jnp.where(col_seq == s, ke_t, 0.0).astype(BF16)
        ds = jnp.dot(lhs, v, preferred_element_type=F32)
        sn_ref[s] = dec_t[:, s * tq:s * tq + 1] * s0 + ds
    og_ref[...] = _rms(o, gn_ref[...]) * _silu(r_ref[...])


def _gla_sample(z, glr, w2p, bg, gn, state, l, nseq, tq, nb=32):
    assert nb * tq == LANES and nseq % nb == 0
    m = z.shape[0]
    return pl.pallas_call(
        functools.partial(_gla_sample_kernel, nb=nb, tq=tq),
        grid=(nseq // nb, GLA_HEADS),
        in_specs=[
            pl.BlockSpec((LANES, GLA_DK), lambda i, h: (i, ZC_QG // GLA_DK + h)),
            pl.BlockSpec((LANES, GLA_DK), lambda i, h: (i, ZC_KG // GLA_DK + h)),
            pl.BlockSpec((LANES, GLA_DV), lambda i, h: (i, ZC_VG // GLA_DV + h)),
            pl.BlockSpec((LANES, GLA_DV), lambda i, h: (i, ZC_RG // GLA_DV + h)),
            pl.BlockSpec((LANES, LANES), lambda i, h: (i, 0)),
            pl.BlockSpec((None, LANES, GLA_DK), lambda i, h: (l, 0, h)),
            pl.BlockSpec((None, 1, GLA_DK), lambda i, h: (l, 0, h)),
            pl.BlockSpec((None, 1, GLA_DV), lambda i, h: (l, 0, 0)),
            pl.BlockSpec((None, nb, None, GLA_DK, GLA_DV), lambda i, h: (l, i, h, 0, 0)),
        ],
        out_specs=[
            pl.BlockSpec((LANES, GLA_DV), lambda i, h: (i, h)),
            pl.BlockSpec((nb, None, GLA_DK, GLA_DV), lambda i, h: (i, h, 0, 0)),
        ],
        out_shape=[jax.ShapeDtypeStruct((m, GLA_HEADS * GLA_DV), F32),
                   jax.ShapeDtypeStruct((nseq, GLA_HEADS, GLA_DK, GLA_DV), F32)],
        compiler_params=_params(("arbitrary", "arbitrary")),
        name="gla_sample",
    )(z, z, z, z, glr, w2p, bg, gn, state)


def _attn_prompt_kernel(slopes_ref, q_ref, k_ref, v_ref, o_ref, acc_ref, m_ref, l_ref,
                        *, t, branches):
    n = LANES
    slope = slopes_ref[pl.program_id(1)]
    scale = ATT_DH ** -0.5
    qi = lax.broadcasted_iota(jnp.int32, (n, n), 0)
    kl = lax.broadcasted_iota(jnp.int32, (n, n), 1)
    d_cur = qi - kl
    valid_cur = d_cur >= 0
    d_prev = d_cur + n
    valid_prev = d_prev <= n

    for bi, (w, d) in enumerate(branches):
        assert w // d == n and t % (n * d) == 0
        nblk = t // (n * d)

        def body(idx, carry, d=d, nblk=nblk, first=(bi == 0)):
            r = idx // nblk
            i = idx - r * nblk
            start = r + i * (n * d)
            pstart = r + jnp.maximum(i - 1, 0) * (n * d)
            if d == 1:
                rows, prows = pl.ds(start, n), pl.ds(pstart, n)
            else:
                rows, prows = pl.ds(start, n, stride=d), pl.ds(pstart, n, stride=d)
            q = q_ref[rows, :].astype(BF16)
            bias_c = d_cur.astype(F32) * (slope * d)
            bias_p = d_prev.astype(F32) * (slope * d)
            s_c = jnp.where(valid_cur, _dot_nt(q, k_ref[rows, :]) * scale - bias_c, NEG_INF)
            s_p = jnp.where(valid_prev & (i > 0),
                            _dot_nt(q, k_ref[prows, :]) * scale - bias_p, NEG_INF)
            m_b = jnp.maximum(jnp.max(s_c, axis=-1, keepdims=True),
                              jnp.max(s_p, axis=-1, keepdims=True))
            p_c = jnp.exp(s_c - m_b)
            p_p = jnp.exp(s_p - m_b)
            l_b = jnp.sum(p_c, axis=-1, keepdims=True) + jnp.sum(p_p, axis=-1, keepdims=True)
            o_b = _dot(p_c, v_ref[rows, :]) + _dot(p_p, v_ref[prows, :])
            if first:
                acc_ref[rows, :] = o_b
                m_ref[rows, :] = jnp.broadcast_to(m_b, (n, n))
                l_ref[rows, :] = jnp.broadcast_to(l_b, (n, n))
            else:
                m_old = m_ref[rows, :]
                m_new = jnp.maximum(m_old, m_b)
                a_old = jnp.exp(m_old - m_new)
                a_b = jnp.exp(m_b - m_new)
                acc_ref[rows, :] = acc_ref[rows, :] * a_old + o_b * a_b
                l_ref[rows, :] = l_ref[rows, :] * a_old + l_b * a_b
                m_ref[rows, :] = m_new
            return carry

        lax.fori_loop(0, t // n, body, 0)

    def fin(c, carry):
        rows = pl.ds(pl.multiple_of(c * 512, 512), 512)
        o_ref[rows, :] = acc_ref[rows, :] / l_ref[rows, :]
        return carry

    lax.fori_loop(0, t // 512, fin, 0)


def _attn_prompt(slopes, z3):
    b, t, _ = z3.shape
    spec = lambda c0: pl.BlockSpec((None, t, ATT_DH), lambda i, h, s: (i, 0, c0 // ATT_DH + h))
    return pl.pallas_call(
        functools.partial(_attn_prompt_kernel, t=t, branches=DILATED_BRANCHES),
        grid_spec=pltpu.PrefetchScalarGridSpec(
            num_scalar_prefetch=1,
            grid=(b, ATT_HEADS),
            in_specs=[spec(ZC_QA), spec(ZC_KA), spec(ZC_VA)],
            out_specs=pl.BlockSpec((None, t, ATT_DH), lambda i, h, s: (i, 0, h)),
            scratch_shapes=[pltpu.VMEM((t, ATT_DH), F32)] * 3,
        ),
        out_shape=jax.ShapeDtypeStruct((b, t, ATT_HEADS * ATT_DH), F32),
        compiler_params=_params(("arbitrary", "arbitrary")),
        name="attn_prompt",
    )(slopes, z3, z3, z3)


def _attn_sample_kernel(sl_ref, q_ref, kn_ref, vn_ref, khi_ref, klo_ref, vhi_ref, vlo_ref,
                        o_ref, *, tq, wbuf, hi0, branches):
    nh = ATT_HEADS
    hsh = int(np.log2(nh))
    rq = tq * nh
    scale = ATT_DH ** -0.5
    q2 = q_ref[...].reshape(rq, ATT_DH).astype(BF16)
    slope = sl_ref[:, :1]

    def iotas(width):
        row = lax.broadcasted_iota(jnp.int32, (rq, width), 0)
        lane = lax.broadcasted_iota(jnp.int32, (rq, width), 1)
        return row >> hsh, (row & (nh - 1)) == (lane & (nh - 1)), lane >> hsh

    def update(state, kc, vc, dist, head_ok):
        m, l, acc = state
        cnt = jnp.zeros(dist.shape, F32)
        for (w, d) in branches:
            cnt = cnt + ((dist & (d - 1)) == 0).astype(F32) * (dist <= w).astype(F32)
        cnt = jnp.where((dist >= 0) & head_ok, cnt, 0.0)
        s = _dot_nt(q2, kc) * scale - slope * dist.astype(F32)
        s = jnp.where(cnt > 0.0, s, NEG_INF)
        m_new = jnp.maximum(m, jnp.max(s, axis=-1, keepdims=True))
        alpha = jnp.exp(m - m_new)
        p = cnt * jnp.exp(s - m_new)
        l = l * alpha + jnp.sum(p, axis=-1, keepdims=True)
        acc = acc * alpha + _dot(p, vc)
        return m_new, l, acc

    state = (jnp.full((rq, 1), NEG_INF, F32), jnp.zeros((rq, 1), F32), jnp.zeros((rq, ATT_DH), F32))
    cp = LANES
    t_row, head_ok, pos_in = iotas(cp * nh)
    for c in range(khi_ref.shape[0] // cp):
        pos = hi0 + c * cp + pos_in
        kc = khi_ref[c * cp:(c + 1) * cp].reshape(cp * nh, ATT_DH)
        vc = vhi_ref[c * cp:(c + 1) * cp].reshape(cp * nh, ATT_DH)
        state = update(state, kc, vc, wbuf + t_row - pos, head_ok)
    g = max(d for _, d in branches)
    nu = klo_ref.shape[1]
    mper = cp // nu
    ush = int(np.log2(nu))
    for c in range(klo_ref.shape[0] // mper):
        pos = g * (c * mper + (pos_in >> ush)) + (pos_in & (nu - 1))
        kc = klo_ref[c * mper:(c + 1) * mper].reshape(cp * nh, ATT_DH)
        vc = vlo_ref[c * mper:(c + 1) * mper].reshape(cp * nh, ATT_DH)
        state = update(state, kc, vc, wbuf + t_row - pos, head_ok)
    t_row, head_ok, pos_in = iotas(rq)
    state = update(state, kn_ref[...].reshape(rq, ATT_DH), vn_ref[...].reshape(rq, ATT_DH),
                   t_row - pos_in, head_ok)
    _, l, acc = state
    o_ref[...] = (acc / l).reshape(tq, nh, ATT_DH)


def _attn_sample(slope_rows, q, kn, vn, cache_k, cache_v, l):
    nseq, tq, nh, dh = q.shape
    depth, _, wbuf, _, _ = cache_k.shape
    g = max(d for _, d in DILATED_BRANCHES)
    w_dense = max(w for w, d in DILATED_BRANCHES if d < g)
    assert nh == ATT_HEADS and dh == ATT_DH and tq <= g and LANES % tq == 0
    assert all(d & (d - 1) == 0 for _, d in DILATED_BRANCHES)
    assert wbuf % w_dense == 0 and wbuf % g == 0 and w_dense % LANES == 0
    assert wbuf >= max(w for w, _ in DILATED_BRANCHES)
    hi0 = wbuf - w_dense
    n_lo = hi0 // g
    assert (n_lo * tq) % LANES == 0
    ck = cache_k.reshape(depth, nseq, wbuf // g, g, nh, dh)
    cv = cache_v.reshape(depth, nseq, wbuf // g, g, nh, dh)
    new = pl.BlockSpec((None, tq, nh, dh), lambda i: (i, 0, 0, 0))
    hi = pl.BlockSpec((None, None, w_dense, nh, dh), lambda i: (l, i, wbuf // w_dense - 1, 0, 0))
    lo = pl.BlockSpec((None, None, n_lo, tq, nh, dh), lambda i: (l, i, 0, 0, 0, 0))
    return pl.pallas_call(
        functools.partial(_attn_sample_kernel, tq=tq, wbuf=wbuf, hi0=hi0, branches=DILATED_BRANCHES),
        grid=(nseq,),
        in_specs=[pl.BlockSpec((tq * nh, LANES), lambda i: (0, 0)), new, new, new, hi, lo, hi, lo],
        out_specs=new,
        out_shape=jax.ShapeDtypeStruct((nseq, tq, nh, dh), F32),
        compiler_params=_params(("arbitrary",)),
        name="attn_sample",
    )(slope_rows, q, kn, vn, cache_k, ck, cache_v, cv)


def _outproj_kernel(x_ref, og_ref, oa_ref, wt_ref, wb_ref, o_ref, lhs_ref, *, half):
    @pl.when(pl.program_id(1) == 0)
    def _():
        lhs_ref[:, :half] = og_ref[...].astype(BF16)
        lhs_ref[:, half:] = oa_ref[...].astype(BF16)

    o_ref[...] = (x_ref[...]
                  + jnp.dot(lhs_ref[:, :half], wt_ref[...].astype(BF16), preferred_element_type=F32)
                  + jnp.dot(lhs_ref[:, half:], wb_ref[...].astype(BF16), preferred_element_type=F32))


def _outproj(x, og, oa, w_out, l, tm, tn=512):
    m, d = x.shape
    half = og.shape[1]
    return pl.pallas_call(
        functools.partial(_outproj_kernel, half=half),
        grid=(m // tm, d // tn),
        in_specs=[
            pl.BlockSpec((tm, tn), lambda i, j: (i, j)),
            pl.BlockSpec((tm, half), lambda i, j: (i, 0)),
            pl.BlockSpec((tm, half), lambda i, j: (i, 0)),
            pl.BlockSpec((None, half, tn), lambda i, j: (l, 0, j)),
            pl.BlockSpec((None, half, tn), lambda i, j: (l, 1, j)),
        ],
        out_specs=pl.BlockSpec((tm, tn), lambda i, j: (i, j)),
        out_shape=jax.ShapeDtypeStruct((m, d), F32),
        scratch_shapes=[pltpu.VMEM((tm, 2 * half), BF16)],
        compiler_params=_params(("arbitrary", "arbitrary")),
        name="outproj",
    )(x, og, oa, w_out, w_out)


def _mlp_kernel(x_ref, g_ref, wu_ref, wd_ref, o_ref, xn_ref):
    @pl.when(pl.program_id(1) == 0)
    def _():
        x = x_ref[...]
        xn_ref[...] = _rms(x, g_ref[...]).astype(BF16)
        o_ref[...] = x

    u = jnp.dot(xn_ref[...], wu_ref[...].astype(BF16), preferred_element_type=F32)
    hdn = jnp.square(jnp.maximum(u, 0.0)).astype(BF16)
    o_ref[...] += jnp.dot(hdn, wd_ref[...].astype(BF16), preferred_element_type=F32)


def _mlp(x, g, w_up, w_down, l, tm, tf=256):
    m, d = x.shape
    ff = w_up.shape[2]
    return pl.pallas_call(
        _mlp_kernel,
        grid=(m // tm, ff // tf),
        in_specs=[
            pl.BlockSpec((tm, d), lambda i, j: (i, 0)),
            pl.BlockSpec((None, 1, d), lambda i, j: (l, 0, 0)),
            pl.BlockSpec((None, d, tf), lambda i, j: (l, 0, j)),
            pl.BlockSpec((None, tf, d), lambda i, j: (l, j, 0)),
        ],
        out_specs=pl.BlockSpec((tm, d), lambda i, j: (i, 0)),
        out_shape=jax.ShapeDtypeStruct((m, d), F32),
        scratch_shapes=[pltpu.VMEM((tm, d), BF16)],
        compiler_params=_params(("arbitrary", "arbitrary")),
        name="mlp",
    )(x, g, w_up, w_down)


def _ple_kernel(xr_ref, xb_ref, g_ref, p_ref, wg_ref, wp_ref, o_ref, xn_ref):
    @pl.when(pl.program_id(1) == 0)
    def _():
        xn_ref[...] = _rms(xr_ref[...], g_ref[...]).astype(BF16)

    gate = jax.nn.sigmoid(jnp.dot(xn_ref[...], wg_ref[...].astype(BF16), preferred_element_type=F32))
    o_ref[...] = xb_ref[...] + gate * _dot(p_ref[...], wp_ref[...])


def _ple(x, g, p, w_gate, w_proj, l, tm, tn=512):
    m, d = x.shape
    pd = p.shape[-1]
    return pl.pallas_call(
        _ple_kernel,
        grid=(m // tm, d // tn),
        in_specs=[
            pl.BlockSpec((tm, d), lambda i, j: (i, 0)),
            pl.BlockSpec((tm, tn), lambda i, j: (i, j)),
            pl.BlockSpec((None, 1, d), lambda i, j: (l, 0, 0)),
            pl.BlockSpec((None, tm, pd), lambda i, j: (l, i, 0)),
            pl.BlockSpec((None, d, tn), lambda i, j: (l, 0, j)),
            pl.BlockSpec((None, pd, tn), lambda i, j: (l, 0, j)),
        ],
        out_specs=pl.BlockSpec((tm, tn), lambda i, j: (i, j)),
        out_shape=jax.ShapeDtypeStruct((m, d), F32),
        scratch_shapes=[pltpu.VMEM((tm, d), BF16)],
        compiler_params=_params(("arbitrary", "arbitrary")),
        name="ple",
    )(x, x, g, p, w_gate, w_proj)


def _final_norm_kernel(x_ref, g_ref, o_ref):
    o_ref[...] = _rms(x_ref[...], g_ref[...])


def _final_norm(x, g, tm):
    m, d = x.shape
    return pl.pallas_call(
        _final_norm_kernel,
        grid=(m // tm,),
        in_specs=[pl.BlockSpec((tm, d), lambda i: (i, 0)),
                  pl.BlockSpec((1, d), lambda i: (0, 0))],
        out_specs=pl.BlockSpec((tm, d), lambda i: (i, 0)),
        out_shape=jax.ShapeDtypeStruct((m, d), F32),
        compiler_params=_params(("arbitrary",)),
        name="final_norm",
    )(x, g)


def kernel(x_prompt, x_sample, cache_k_win, cache_v_win, state_gla, p_prompt, p_sample,
           norm_mix, w_in, w_gate2, b_gate, gla_norm, w_out, norm_mlp, w_up, w_down,
           norm_ple, w_ple_gate, w_ple_proj, norm_final):
    bp, tp, d = x_prompt.shape
    bs, ts, _ = x_sample.shape
    depth = w_in.shape[0]
    mp, ms = bp * tp, bs * ts
    tm_p, tm_s = 1024, ms
    w_keep = min(MAX_WINDOW, tp)

    slopes_np = (2.0 ** (-8.0 * np.arange(1, ATT_HEADS + 1) / ATT_HEADS)).astype(np.float32)
    slopes = jnp.asarray(slopes_np)
    slope_rows = jnp.asarray(np.broadcast_to(np.tile(slopes_np, ts)[:, None], (ts * ATT_HEADS, LANES)))
    lr0 = GLA_PART
    w_att = w_in[:, :, lr0 + GATE_RANK:]
    w_lr = jnp.pad(w_in[:, :, lr0:lr0 + GATE_RANK], ((0, 0), (0, 0), (0, LANES - GATE_RANK)))
    w2p = jnp.pad(w_gate2, ((0, 0), (0, LANES - GATE_RANK), (0, 0)))
    row = lambda a: a.reshape(depth, 1, a.shape[-1])
    nmix, nmlp, nple, bg, gn = row(norm_mix), row(norm_mlp), row(norm_ple), row(b_gate), row(gla_norm)
    pp = p_prompt.reshape(depth, mp, -1)
    ps = p_sample.reshape(depth, ms, -1)

    hp = x_prompt.reshape(mp, d)
    hs = x_sample.reshape(ms, d)
    kp_l, vp_l, gp_l, ks_l, vs_l, gs_l = [], [], [], [], [], []
    for l in range(depth):
        z, glr = _inproj(hp, nmix, w_in, w_att, w_lr, l, tm_p)
        z3 = z.reshape(bp, tp, Z_COLS)
        og, s_new = _gla_prompt(z3, glr.reshape(bp, tp, LANES), w2p, bg, gn, l)
        oa = _attn_prompt(slopes, z3)
        kp_l.append(z3[:, tp - w_keep:, ZC_KA:ZC_VA].reshape(bp, w_keep, ATT_HEADS, ATT_DH))
        vp_l.append(z3[:, tp - w_keep:, ZC_VA:].reshape(bp, w_keep, ATT_HEADS, ATT_DH))
        gp_l.append(s_new)
        hp = _outproj(hp, og.reshape(mp, -1), oa.reshape(mp, -1), w_out, l, tm_p)
        hp = _mlp(hp, nmlp, w_up, w_down, l, tm_p)
        hp = _ple(hp, nple, pp, w_ple_gate, w_ple_proj, l, tm_p)
        z, glr = _inproj(hs, nmix, w_in, w_att, w_lr, l, tm_s)
        og, s_new = _gla_sample(z, glr, w2p, bg, gn, state_gla, l, bs, ts)
        heads = lambda c0: z[:, c0:c0 + ATT_HEADS * ATT_DH].reshape(bs, ts, ATT_HEADS, ATT_DH)
        q_s, k_s, v_s = heads(ZC_QA), heads(ZC_KA), heads(ZC_VA)
        oa = _attn_sample(slope_rows, q_s, k_s, v_s, cache_k_win, cache_v_win, l)
        ks_l.append(k_s)
        vs_l.append(v_s)
        gs_l.append(s_new)
        hs = _outproj(hs, og, oa.reshape(ms, -1), w_out, l, tm_s)
        hs = _mlp(hs, nmlp, w_up, w_down, l, tm_s)
        hs = _ple(hs, nple, ps, w_ple_gate, w_ple_proj, l, tm_s)
    gfin = norm_final.reshape(1, d)
    y_prompt = _final_norm(hp, gfin, tm_p).reshape(bp, tp, d)
    y_sample = _final_norm(hs, gfin, tm_s).reshape(bs, ts, d)
    return (y_prompt, y_sample, jnp.stack(kp_l), jnp.stack(vp_l), jnp.stack(gp_l),
            jnp.stack(ks_l), jnp.stack(vs_l), jnp.stack(gs_l))
```

```python
import functools

import jax
import jax.numpy as jnp
import numpy as np
from jax import lax
from jax.experimental import pallas as pl
from jax.experimental.pallas import tpu as pltpu

F32 = jnp.float32
BF16 = jnp.bfloat16

EPS = 1e-6
NEG_INF = -1e30
GLA_HEADS = 4
GLA_DK = 128
GLA_DV = 256
GLA_TAU = 16.0
GLA_CHUNK = 64
ATT_HEADS = 8
ATT_DH = 128
DILATED_BRANCHES = ((128, 1), (512, 4), (2048, 16))
MAX_WINDOW = 2048
GATE_RANK = 16

LANES = 128
VMEM_LIMIT = 56 * 1024 * 1024

Z_COLS = 6144
GLA_PART = 3072
ZC_QG, ZC_KG, ZC_VG, ZC_RG, ZC_QA, ZC_KA, ZC_VA = 0, 512, 1024, 2048, 3072, 4096, 5120


def _params(sem, vmem=VMEM_LIMIT):
    return pltpu.CompilerParams(dimension_semantics=sem, vmem_limit_bytes=vmem)


def _rms(x, g):
    return x * lax.rsqrt(jnp.mean(x * x, axis=-1, keepdims=True) + EPS) * g


def _dot(a, b):
    return jnp.dot(a.astype(BF16), b.astype(BF16), preferred_element_type=F32)


def _dot_nt(a, b):
    return lax.dot_general(a.astype(BF16), b.astype(BF16), (((1,), (1,)), ((), ())),
                           preferred_element_type=F32)


def _split2(x):
    hi = x.astype(BF16)
    lo = (x - hi.astype(F32)).astype(BF16)
    return hi, lo


def _dot_hi(a, b):
    ah, al = _split2(a)
    bh, bl = _split2(b)
    d = lambda u, v: jnp.dot(u, v, preferred_element_type=F32)
    return d(ah, bh) + d(ah, bl) + d(al, bh)


def _dot_exact(m, x):
    hi = x.astype(BF16)
    r1 = x - hi.astype(F32)
    mid = r1.astype(BF16)
    lo = (r1 - mid.astype(F32)).astype(BF16)
    d = lambda v: jnp.dot(m, v, preferred_element_type=F32)
    return d(hi) + d(mid) + d(lo)


def _log_sigmoid(x):
    return jnp.minimum(x, 0.0) - jnp.log1p(jnp.exp(-jnp.abs(x)))


def _silu(x):
    return x * jax.nn.sigmoid(x)


def _inproj_kernel(x_ref, g_ref, wa_ref, wb_ref, wl_ref, z_ref, glr_ref, xn_ref, *, nja):
    j = pl.program_id(1)

    @pl.when(j == 0)
    def _():
        xb = _rms(x_ref[...], g_ref[...]).astype(BF16)
        xn_ref[...] = xb
        glr_ref[...] = _dot_nt(xb, wl_ref[...])

    @pl.when(j < nja)
    def _():
        z_ref[...] = _dot_nt(xn_ref[...], wa_ref[...])

    @pl.when(j >= nja)
    def _():
        z_ref[...] = _dot_nt(xn_ref[...], wb_ref[...])


def _inproj(x, g, w_t, w_att, w_lr, l, tm, tn=1024):
    m, d = x.shape
    assert GLA_PART % tn == 0 and Z_COLS % tn == 0
    nja = GLA_PART // tn
    return pl.pallas_call(
        functools.partial(_inproj_kernel, nja=nja),
        grid=(m // tm, Z_COLS // tn),
        in_specs=[
            pl.BlockSpec((tm, d), lambda i, j: (i, 0)),
            pl.BlockSpec((None, 1, d), lambda i, j: (l, 0, 0)),
            pl.BlockSpec((None, tn, d), lambda i, j: (l, jnp.minimum(j, nja - 1), 0)),
            pl.BlockSpec((None, tn, d), lambda i, j: (l, jnp.maximum(j - nja, 0), 0)),
            pl.BlockSpec((None, LANES, d), lambda i, j: (l, 0, 0)),
        ],
        out_specs=[
            pl.BlockSpec((tm, tn), lambda i, j: (i, j)),
            pl.BlockSpec((tm, LANES), lambda i, j: (i, 0)),
        ],
        out_shape=[jax.ShapeDtypeStruct((m, Z_COLS), F32),
                   jax.ShapeDtypeStruct((m, LANES), F32)],
        scratch_shapes=[pltpu.VMEM((tm, d), BF16)],
        compiler_params=_params(("arbitrary", "arbitrary")),
        name="inproj",
    )(x, g, w_t, w_att, w_lr)


def _gla_prompt_kernel(q_ref, k_ref, v_ref, r_ref, glr_ref, w2_ref, bg_ref, gn_ref,
                       og_ref, sout_ref, s_ref, *, tb, chunk):
    step = pl.program_id(1)

    @pl.when(step == 0)
    def _():
        s_ref[...] = jnp.zeros_like(s_ref)

    c = chunk
    ri = lax.broadcasted_iota(jnp.int32, (c, c), 0)
    ci = lax.broadcasted_iota(jnp.int32, (c, c), 1)
    causal = ci <= ri
    tri = causal.astype(BF16)
    w2 = w2_ref[...]
    bg = bg_ref[...]
    gn = gn_ref[...]
    zpad_k = jnp.zeros((LANES - c, GLA_DK), F32)
    zpad_v = jnp.zeros((LANES - c, GLA_DV), BF16)

    def body(n, carry):
        r0 = pl.multiple_of(n * c, c)
        rows = pl.ds(r0, c)
        x = _dot_hi(glr_ref[rows, :], w2) + bg
        g = _log_sigmoid(x) * (1.0 / GLA_TAU)
        b = _dot_exact(tri, g)
        for h in range(GLA_HEADS):
            ks = slice(h * GLA_DK, (h + 1) * GLA_DK)
            vs = slice(h * GLA_DV, (h + 1) * GLA_DV)
            bh = b[:, ks]
            blast = bh[c - 1:c, :]
            q = q_ref[rows, ks] * (GLA_DK ** -0.5)
            k = k_ref[rows, ks]
            v = v_ref[rows, vs].astype(BF16)
            qd = (q * jnp.exp(bh)).astype(BF16)
            kd = k * jnp.exp(-bh)
            ke = k * jnp.exp(blast - bh)
            a = jnp.where(causal, _dot_nt(qd, kd), 0.0)
            s_old = s_ref[h]
            o = _dot(a, v) + _dot(qd, s_old)
            ke_t = jnp.concatenate([ke, zpad_k], axis=0).T.astype(BF16)
            v_pad = jnp.concatenate([v, zpad_v], axis=0)
            ds = jnp.dot(ke_t, v_pad, preferred_element_type=F32)
            dec = jnp.broadcast_to(jnp.exp(blast), (GLA_DK, GLA_DK)).T
            s_ref[h] = jnp.concatenate([dec, dec], axis=1) * s_old + ds
            on = _rms(o, gn)
            og_ref[rows, vs] = (on * _silu(r_ref[rows, vs])).astype(og_ref.dtype)
        return carry

    lax.fori_loop(0, tb // c, body, 0)

    @pl.when(step == pl.num_programs(1) - 1)
    def _():
        sout_ref[...] = s_ref[...]


def _gla_prompt(z3, glr3, w2p, bg, gn, l, tb=512):
    b, t, _ = z3.shape
    chunk = np.gcd(t, GLA_CHUNK)
    assert t % tb == 0 and tb % chunk == 0
    qk_w = GLA_HEADS * GLA_DK
    v_w = GLA_HEADS * GLA_DV
    return pl.pallas_call(
        functools.partial(_gla_prompt_kernel, tb=tb, chunk=int(chunk)),
        grid=(b, t // tb),
        in_specs=[
            pl.BlockSpec((None, tb, qk_w), lambda i, s: (i, s, ZC_QG // qk_w)),
            pl.BlockSpec((None, tb, qk_w), lambda i, s: (i, s, ZC_KG // qk_w)),
            pl.BlockSpec((None, tb, v_w), lambda i, s: (i, s, ZC_VG // v_w)),
            pl.BlockSpec((None, tb, v_w), lambda i, s: (i, s, ZC_RG // v_w)),
            pl.BlockSpec((None, tb, LANES), lambda i, s: (i, s, 0)),
            pl.BlockSpec((None, LANES, qk_w), lambda i, s: (l, 0, 0)),
            pl.BlockSpec((None, 1, qk_w), lambda i, s: (l, 0, 0)),
            pl.BlockSpec((None, 1, GLA_DV), lambda i, s: (l, 0, 0)),
        ],
        out_specs=[
            pl.BlockSpec((None, tb, v_w), lambda i, s: (i, s, 0)),
            pl.BlockSpec((None, GLA_HEADS, GLA_DK, GLA_DV), lambda i, s: (i, 0, 0, 0)),
        ],
        out_shape=[jax.ShapeDtypeStruct((b, t, v_w), BF16),
                   jax.ShapeDtypeStruct((b, GLA_HEADS, GLA_DK, GLA_DV), F32)],
        scratch_shapes=[pltpu.VMEM((GLA_HEADS, GLA_DK, GLA_DV), F32)],
        compiler_params=_params(("arbitrary", "arbitrary")),
        name="gla_prompt",
    )(z3, z3, z3, z3, glr3, w2p, bg, gn)


def _gla_sample_kernel(q_ref, k_ref, v_ref, r_ref, glr_ref, w2_ref, bg_ref, gn_ref, s0_ref,
                       og_ref, sn_ref, *, nb, tq):
    rws = nb * tq
    sh = int(np.log2(tq))
    ri = lax.broadcasted_iota(jnp.int32, (rws, rws), 0)
    ci = lax.broadcasted_iota(jnp.int32, (rws, rws), 1)
    same = (ri >> sh) == (ci >> sh)
    causal = same & (ci <= ri)
    x = _dot_hi(glr_ref[...], w2_ref[...]) + bg_ref[...]
    g = _log_sigmoid(x) * (1.0 / GLA_TAU)
    b = _dot_exact(causal.astype(BF16), g)
    bl = _dot_exact(same.astype(BF16), g)
    q = q_ref[...] * (GLA_DK ** -0.5)
    k = k_ref[...]
    qd = (q * jnp.exp(b)).astype(BF16)
    kd = k * jnp.exp(-b)
    ke = k * jnp.exp(bl - b)
    v = v_ref[...].astype(BF16)
    a = jnp.where(causal, _dot_nt(qd, kd), 0.0)
    o = _dot(a, v)
    ke_t = ke.T
    dec_t = jnp.exp(bl).T
    row_seq = lax.broadcasted_iota(jnp.int32, (rws, GLA_DV), 0) >> sh
    col_seq = lax.broadcasted_iota(jnp.int32, (GLA_DK, rws), 1) >> sh
    for s in range(nb):
        s0 = s0_ref[s]
        o = o + jnp.where(row_seq == s, _dot(qd, s0), 0.0)
        lhs = jnp.where(col_seq == s, ke_t, 0.0).astype(BF16)
        ds = jnp.dot(lhs, v, preferred_element_type=F32)
        sn_ref[s] = dec_t[:, s * tq:s * tq + 1] * s0 + ds
    og_ref[...] = (_rms(o, gn_ref[...]) * _silu(r_ref[...])).astype(og_ref.dtype)


def _gla_sample(z, glr, w2p, bg, gn, state, l, nseq, tq, nb=32):
    assert nb * tq == LANES and nseq % nb == 0
    m = z.shape[0]
    return pl.pallas_call(
        functools.partial(_gla_sample_kernel, nb=nb, tq=tq),
        grid=(nseq // nb, GLA_HEADS),
        in_specs=[
            pl.BlockSpec((LANES, GLA_DK), lambda i, h: (i, ZC_QG // GLA_DK + h)),
            pl.BlockSpec((LANES, GLA_DK), lambda i, h: (i, ZC_KG // GLA_DK + h)),
            pl.BlockSpec((LANES, GLA_DV), lambda i, h: (i, ZC_VG // GLA_DV + h)),
            pl.BlockSpec((LANES, GLA_DV), lambda i, h: (i, ZC_RG // GLA_DV + h)),
            pl.BlockSpec((LANES, LANES), lambda i, h: (i, 0)),
            pl.BlockSpec((None, LANES, GLA_DK), lambda i, h: (l, 0, h)),
            pl.BlockSpec((None, 1, GLA_DK), lambda i, h: (l, 0, h)),
            pl.BlockSpec((None, 1, GLA_DV), lambda i, h: (l, 0, 0)),
            pl.BlockSpec((None, nb, None, GLA_DK, GLA_DV), lambda i, h: (l, i, h, 0, 0)),
        ],
        out_specs=[
            pl.BlockSpec((LANES, GLA_DV), lambda i, h: (i, h)),
            pl.BlockSpec((nb, None, GLA_DK, GLA_DV), lambda i, h: (i, h, 0, 0)),
        ],
        out_shape=[jax.ShapeDtypeStruct((m, GLA_HEADS * GLA_DV), BF16),
                   jax.ShapeDtypeStruct((nseq, GLA_HEADS, GLA_DK, GLA_DV), F32)],
        compiler_params=_params(("arbitrary", "arbitrary")),
        name="gla_sample",
    )(z, z, z, z, glr, w2p, bg, gn, state)


def _attn_prompt_kernel(slopes_ref, q_ref, k_ref, v_ref, o_ref, acc_ref, m_ref, l_ref,
                        *, t, branches, unroll):
    n = LANES
    slope = slopes_ref[pl.program_id(1)]
    scale = ATT_DH ** -0.5
    qi = lax.broadcasted_iota(jnp.int32, (n, n), 0)
    kl = lax.broadcasted_iota(jnp.int32, (n, n), 1)
    d_cur = qi - kl
    valid_cur = d_cur >= 0
    d_prev = d_cur + n
    valid_prev = d_prev <= n
    ones = jnp.ones((n, n), BF16)

    for bi, (w, d) in enumerate(branches):
        assert w // d == n and t % (n * d) == 0
        nblk = t // (n * d)
        bias_c = d_cur.astype(F32) * (slope * d)
        bias_p = d_prev.astype(F32) * (slope * d)

        def body(idx, carry, d=d, nblk=nblk, first=(bi == 0), bias_c=bias_c, bias_p=bias_p):
            r = idx // nblk
            i = idx - r * nblk
            start = r + i * (n * d)
            pstart = r + jnp.maximum(i - 1, 0) * (n * d)
            if d == 1:
                rows, prows = pl.ds(start, n), pl.ds(pstart, n)
            else:
                rows, prows = pl.ds(start, n, stride=d), pl.ds(pstart, n, stride=d)
            q = q_ref[rows, :].astype(BF16)
            s_c = jnp.where(valid_cur, _dot_nt(q, k_ref[rows, :]) * scale - bias_c, NEG_INF)
            s_p = jnp.where(valid_prev & (i > 0),
                            _dot_nt(q, k_ref[prows, :]) * scale - bias_p, NEG_INF)
            m_b = jnp.max(jnp.maximum(s_c, s_p), axis=-1, keepdims=True)
            p_c = jnp.exp(s_c - m_b).astype(BF16)
            p_p = jnp.exp(s_p - m_b).astype(BF16)
            v_c = jnp.concatenate([v_ref[rows, :].astype(BF16), ones], axis=1)
            v_p = jnp.concatenate([v_ref[prows, :].astype(BF16), ones], axis=1)
            ol = (jnp.dot(p_c, v_c, preferred_element_type=F32)
                  + jnp.dot(p_p, v_p, preferred_element_type=F32))
            o_b, l_b = ol[:, :n], ol[:, n:]
            if first:
                acc_ref[rows, :] = o_b
                m_ref[rows, :] = jnp.broadcast_to(m_b, (n, n))
                l_ref[rows, :] = l_b
            else:
                m_old = m_ref[rows, :]
                m_new = jnp.maximum(m_old, m_b)
                a_old = jnp.exp(m_old - m_new)
                a_b = jnp.exp(m_b - m_new)
                acc_ref[rows, :] = acc_ref[rows, :] * a_old + o_b * a_b
                l_ref[rows, :] = l_ref[rows, :] * a_old + l_b * a_b
                m_ref[rows, :] = m_new
            return carry

        lax.fori_loop(0, t // n, body, 0, unroll=unroll)

    def fin(c, carry):
        rows = pl.ds(pl.multiple_of(c * 512, 512), 512)
        o_ref[rows, :] = (acc_ref[rows, :] / l_ref[rows, :]).astype(o_ref.dtype)
        return carry

    lax.fori_loop(0, t // 512, fin, 0)


def _attn_prompt(slopes, z3, unroll=8):
    b, t, _ = z3.shape
    spec = lambda c0: pl.BlockSpec((None, t, ATT_DH), lambda i, h, s: (i, 0, c0 // ATT_DH + h))
    return pl.pallas_call(
        functools.partial(_attn_prompt_kernel, t=t, branches=DILATED_BRANCHES, unroll=unroll),
        grid_spec=pltpu.PrefetchScalarGridSpec(
            num_scalar_prefetch=1,
            grid=(b, ATT_HEADS),
            in_specs=[spec(ZC_QA), spec(ZC_KA), spec(ZC_VA)],
            out_specs=pl.BlockSpec((None, t, ATT_DH), lambda i, h, s: (i, 0, h)),
            scratch_shapes=[pltpu.VMEM((t, ATT_DH), F32)] * 3,
        ),
        out_shape=jax.ShapeDtypeStruct((b, t, ATT_HEADS * ATT_DH), BF16),
        compiler_params=_params(("arbitrary", "arbitrary")),
        name="attn_prompt",
    )(slopes, z3, z3, z3)


def _attn_sample_kernel(sl_ref, q_ref, kn_ref, vn_ref, khi_ref, klo_ref, vhi_ref, vlo_ref,
                        o_ref, cnt_ref, bias_ref, *, tq, wbuf, hi0, branches):
    nh = ATT_HEADS
    hsh = int(np.log2(nh))
    rq = tq * nh
    cp = LANES
    scale = ATT_DH ** -0.5
    slope = sl_ref[:, :1]
    n_hi = khi_ref.shape[0] // cp
    g = max(d for _, d in branches)
    nu = klo_ref.shape[1]
    mper = cp // nu
    ush = int(np.log2(nu))
    n_lo = klo_ref.shape[0] // mper

    def iotas(width):
        row = lax.broadcasted_iota(jnp.int32, (rq, width), 0)
        lane = lax.broadcasted_iota(jnp.int32, (rq, width), 1)
        return row >> hsh, (row & (nh - 1)) == (lane & (nh - 1)), lane >> hsh

    def multiplicity(dist, head_ok):
        cnt = jnp.zeros(dist.shape, F32)
        for (w, d) in branches:
            cnt = cnt + ((dist & (d - 1)) == 0).astype(F32) * (dist <= w).astype(F32)
        return jnp.where((dist >= 0) & head_ok, cnt, 0.0)

    @pl.when(pl.program_id(0) == 0)
    def _():
        t_row, head_ok, pos_in = iotas(cp * nh)
        for c in range(n_hi):
            dist = wbuf + t_row - (hi0 + c * cp + pos_in)
            cnt_ref[c] = multiplicity(dist, head_ok)
            bias_ref[c] = slope * dist.astype(F32)
        for c in range(n_lo):
            pos = g * (c * mper + (pos_in >> ush)) + (pos_in & (nu - 1))
            dist = wbuf + t_row - pos
            cnt_ref[n_hi + c] = multiplicity(dist, head_ok)
            bias_ref[n_hi + c] = slope * dist.astype(F32)

    q2 = q_ref[...].reshape(rq, ATT_DH).astype(BF16)

    def slab(ref, c, per):
        return ref[c * per:(c + 1) * per].reshape(cp * nh, ATT_DH)

    def score(kc, cnt, bias):
        return jnp.where(cnt > 0.0, _dot_nt(q2, kc) * scale - bias, NEG_INF)

    keys = [slab(khi_ref, c, cp) for c in range(n_hi)] + [slab(klo_ref, c, mper) for c in range(n_lo)]
    vals = [slab(vhi_ref, c, cp) for c in range(n_hi)] + [slab(vlo_ref, c, mper) for c in range(n_lo)]
    scores = [score(kc, cnt_ref[c], bias_ref[c]) for c, kc in enumerate(keys)]
    t_row, head_ok, pos_in = iotas(rq)
    dist = t_row - pos_in
    cnt_new = multiplicity(dist, head_ok)
    s_new = score(kn_ref[...].reshape(rq, ATT_DH), cnt_new, slope * dist.astype(F32))
    m_all = scores[0]
    for s in scores[1:]:
        m_all = jnp.maximum(m_all, s)
    m = jnp.maximum(jnp.max(m_all, axis=-1, keepdims=True), jnp.max(s_new, axis=-1, keepdims=True))
    p_new = cnt_new * jnp.exp(s_new - m)
    acc = _dot(p_new, vn_ref[...].reshape(rq, ATT_DH))
    psum = jnp.zeros((rq, cp * nh), F32)
    for c, (s, vc) in enumerate(zip(scores, vals)):
        p = cnt_ref[c] * jnp.exp(s - m)
        psum = psum + p
        acc = acc + _dot(p, vc)
    l = jnp.sum(psum, axis=-1, keepdims=True) + jnp.sum(p_new, axis=-1, keepdims=True)
    o_ref[...] = (acc / l).reshape(tq, nh, ATT_DH)


def _attn_sample(slope_rows, q, kn, vn, cache_k, cache_v, l):
    nseq, tq, nh, dh = q.shape
    depth, _, wbuf, _, _ = cache_k.shape
    g = max(d for _, d in DILATED_BRANCHES)
    w_dense = max(w for w, d in DILATED_BRANCHES if d < g)
    assert nh == ATT_HEADS and dh == ATT_DH and tq <= g and LANES % tq == 0
    assert all(d & (d - 1) == 0 for _, d in DILATED_BRANCHES)
    assert wbuf % w_dense == 0 and wbuf % g == 0 and w_dense % LANES == 0
    assert wbuf >= max(w for w, _ in DILATED_BRANCHES)
    hi0 = wbuf - w_dense
    n_lo = hi0 // g
    assert (n_lo * tq) % LANES == 0
    n_chunks = w_dense // LANES + n_lo * tq // LANES
    ck = cache_k.reshape(depth, nseq, wbuf // g, g, nh, dh)
    cv = cache_v.reshape(depth, nseq, wbuf // g, g, nh, dh)
    new = pl.BlockSpec((None, tq, nh, dh), lambda i: (i, 0, 0, 0))
    hi = pl.BlockSpec((None, None, w_dense, nh, dh), lambda i: (l, i, wbuf // w_dense - 1, 0, 0))
    lo = pl.BlockSpec((None, None, n_lo, tq, nh, dh), lambda i: (l, i, 0, 0, 0, 0))
    return pl.pallas_call(
        functools.partial(_attn_sample_kernel, tq=tq, wbuf=wbuf, hi0=hi0, branches=DILATED_BRANCHES),
        grid=(nseq,),
        in_specs=[pl.BlockSpec((tq * nh, LANES), lambda i: (0, 0)), new, new, new, hi, lo, hi, lo],
        out_specs=new,
        out_shape=jax.ShapeDtypeStruct((nseq, tq, nh, dh), F32),
        scratch_shapes=[pltpu.VMEM((n_chunks, tq * nh, LANES * nh), F32)] * 2,
        compiler_params=_params(("arbitrary",)),
        name="attn_sample",
    )(slope_rows, q, kn, vn, cache_k, ck, cache_v, cv)


def _outproj_kernel(x_ref, og_ref, oa_ref, wt_ref, wb_ref, o_ref):
    o_ref[...] = (x_ref[...]
                  + jnp.dot(og_ref[...], wt_ref[...], preferred_element_type=F32)
                  + jnp.dot(oa_ref[...], wb_ref[...], preferred_element_type=F32))


def _outproj(x, og, oa, w_out, l, tm, tn=512):
    m, d = x.shape
    half = og.shape[1]
    return pl.pallas_call(
        _outproj_kernel,
        grid=(m // tm, d // tn),
        in_specs=[
            pl.BlockSpec((tm, tn), lambda i, j: (i, j)),
            pl.BlockSpec((tm, half), lambda i, j: (i, 0)),
            pl.BlockSpec((tm, half), lambda i, j: (i, 0)),
            pl.BlockSpec((None, half, tn), lambda i, j: (l, 0, j)),
            pl.BlockSpec((None, half, tn), lambda i, j: (l, 1, j)),
        ],
        out_specs=pl.BlockSpec((tm, tn), lambda i, j: (i, j)),
        out_shape=jax.ShapeDtypeStruct((m, d), F32),
        compiler_params=_params(("arbitrary", "arbitrary")),
        name="outproj",
    )(x, og, oa, w_out, w_out)


def _mlp_kernel(x_ref, g_ref, wu_ref, wd_ref, o_ref, xn_ref, h_ref, *, na, tf, tn):
    j = pl.program_id(1)

    @pl.when(j == 0)
    def _():
        xn_ref[...] = _rms(x_ref[...], g_ref[...]).astype(BF16)

    @pl.when(j < na)
    def _():
        u = jnp.dot(xn_ref[...], wu_ref[...], preferred_element_type=F32)
        h_ref[j] = jnp.square(jnp.maximum(u, 0.0)).astype(BF16)

    @pl.when(j >= na)
    def _():
        cols = pl.ds(pl.multiple_of((j - na) * tn, tn), tn)
        acc = x_ref[:, cols]
        for c in range(na):
            acc = acc + jnp.dot(h_ref[c], wd_ref[c * tf:(c + 1) * tf, :], preferred_element_type=F32)
        o_ref[...] = acc


def _mlp(x, g, w_up, w_down, l, tm, tf=512, tn=512):
    m, d = x.shape
    ff = w_up.shape[2]
    na, nb = ff // tf, d // tn
    return pl.pallas_call(
        functools.partial(_mlp_kernel, na=na, tf=tf, tn=tn),
        grid=(m // tm, na + nb),
        in_specs=[
            pl.BlockSpec((tm, d), lambda i, j: (i, 0), pipeline_mode=pl.Buffered(1)),
            pl.BlockSpec((None, 1, d), lambda i, j: (l, 0, 0)),
            pl.BlockSpec((None, d, tf), lambda i, j: (l, 0, jnp.minimum(j, na - 1))),
            pl.BlockSpec((None, ff, tn), lambda i, j: (l, 0, jnp.maximum(j - na, 0))),
        ],
        out_specs=pl.BlockSpec((tm, tn), lambda i, j: (i, jnp.maximum(j - na, 0))),
        out_shape=jax.ShapeDtypeStruct((m, d), F32),
        scratch_shapes=[pltpu.VMEM((tm, d), BF16), pltpu.VMEM((na, tm, tf), BF16)],
        compiler_params=_params(("arbitrary", "arbitrary")),
        name="mlp",
    )(x, g, w_up, w_down)


def _ple_kernel(x_ref, g_ref, p_ref, wg_ref, wp_ref, gf_ref, o_ref, xn_ref, *, tn, final):
    j = pl.program_id(1)

    @pl.when(j == 0)
    def _():
        xn_ref[...] = _rms(x_ref[...], g_ref[...]).astype(BF16)

    cols = pl.ds(pl.multiple_of(j * tn, tn), tn)
    gate = jax.nn.sigmoid(jnp.dot(xn_ref[...], wg_ref[...], preferred_element_type=F32))
    o_ref[:, cols] = x_ref[:, cols] + gate * _dot(p_ref[...], wp_ref[...])

    if final:
        @pl.when(j == pl.num_programs(1) - 1)
        def _():
            o_ref[...] = _rms(o_ref[...], gf_ref[...])


def _ple(x, g, p, w_gate, w_proj, g_final, l, tm, final, tn=512):
    m, d = x.shape
    pd = p.shape[-1]
    return pl.pallas_call(
        functools.partial(_ple_kernel, tn=tn, final=final),
        grid=(m // tm, d // tn),
        in_specs=[
            pl.BlockSpec((tm, d), lambda i, j: (i, 0)),
            pl.BlockSpec((None, 1, d), lambda i, j: (l, 0, 0)),
            pl.BlockSpec((None, tm, pd), lambda i, j: (l, i, 0)),
            pl.BlockSpec((None, d, tn), lambda i, j: (l, 0, j)),
            pl.BlockSpec((None, pd, tn), lambda i, j: (l, 0, j)),
            pl.BlockSpec((1, d), lambda i, j: (0, 0)),
        ],
        out_specs=pl.BlockSpec((tm, d), lambda i, j: (i, 0)),
        out_shape=jax.ShapeDtypeStruct((m, d), F32),
        scratch_shapes=[pltpu.VMEM((tm, d), BF16)],
        compiler_params=_params(("arbitrary", "arbitrary")),
        name="ple",
    )(x, g, p, w_gate, w_proj, g_final)


def kernel(x_prompt, x_sample, cache_k_win, cache_v_win, state_gla, p_prompt, p_sample,
           norm_mix, w_in, w_gate2, b_gate, gla_norm, w_out, norm_mlp, w_up, w_down,
           norm_ple, w_ple_gate, w_ple_proj, norm_final):
    bp, tp, d = x_prompt.shape
    bs, ts, _ = x_sample.shape
    depth = w_in.shape[0]
    mp, ms = bp * tp, bs * ts
    tm_p, tm_s = 1024, ms
    w_keep = min(MAX_WINDOW, tp)

    slopes_np = (2.0 ** (-8.0 * np.arange(1, ATT_HEADS + 1) / ATT_HEADS)).astype(np.float32)
    slopes = jnp.asarray(slopes_np)
    slope_rows = jnp.asarray(np.broadcast_to(np.tile(slopes_np, ts)[:, None], (ts * ATT_HEADS, LANES)))
    lr0 = GLA_PART
    w_z = jnp.swapaxes(w_in, 1, 2).astype(BF16)
    w_att = w_z[:, lr0 + GATE_RANK:]
    w_lr = jnp.pad(w_z[:, lr0:lr0 + GATE_RANK], ((0, 0), (0, LANES - GATE_RANK), (0, 0)))
    w2p = jnp.pad(w_gate2, ((0, 0), (0, LANES - GATE_RANK), (0, 0)))
    w_out_b, w_up_b, w_down_b = w_out.astype(BF16), w_up.astype(BF16), w_down.astype(BF16)
    w_pg_b, w_pp_b = w_ple_gate.astype(BF16), w_ple_proj.astype(BF16)
    row = lambda a: a.reshape(depth, 1, a.shape[-1])
    nmix, nmlp, nple, bg, gn = row(norm_mix), row(norm_mlp), row(norm_ple), row(b_gate), row(gla_norm)
    gfin = norm_final.reshape(1, d)
    pp = p_prompt.reshape(depth, mp, -1)
    ps = p_sample.reshape(depth, ms, -1)

    hp = x_prompt.reshape(mp, d)
    hs = x_sample.reshape(ms, d)
    kp_l, vp_l, gp_l, ks_l, vs_l, gs_l = [], [], [], [], [], []
    for l in range(depth):
        last = l == depth - 1
        z, glr = _inproj(hp, nmix, w_z, w_att, w_lr, l, tm_p)
        z3 = z.reshape(bp, tp, Z_COLS)
        og, s_new = _gla_prompt(z3, glr.reshape(bp, tp, LANES), w2p, bg, gn, l)
        oa = _attn_prompt(slopes, z3)
        kp_l.append(z3[:, tp - w_keep:, ZC_KA:ZC_VA].reshape(bp, w_keep, ATT_HEADS, ATT_DH))
        vp_l.append(z3[:, tp - w_keep:, ZC_VA:].reshape(bp, w_keep, ATT_HEADS, ATT_DH))
        gp_l.append(s_new)
        hp = _outproj(hp, og.reshape(mp, -1), oa.reshape(mp, -1), w_out_b, l, tm_p)
        hp = _mlp(hp, nmlp, w_up_b, w_down_b, l, tm_p)
        hp = _ple(hp, nple, pp, w_pg_b, w_pp_b, gfin, l, tm_p, last)
        z, glr = _inproj(hs, nmix, w_z, w_att, w_lr, l, tm_s)
        og, s_new = _gla_sample(z, glr, w2p, bg, gn, state_gla, l, bs, ts)
        heads = lambda c0: z[:, c0:c0 + ATT_HEADS * ATT_DH].reshape(bs, ts, ATT_HEADS, ATT_DH)
        q_s, k_s, v_s = heads(ZC_QA), heads(ZC_KA), heads(ZC_VA)
        oa = _attn_sample(slope_rows, q_s, k_s, v_s, cache_k_win, cache_v_win, l)
        ks_l.append(k_s)
        vs_l.append(v_s)
        gs_l.append(s_new)
        hs = _outproj(hs, og, oa.reshape(ms, -1).astype(BF16), w_out_b, l, tm_s)
        hs = _mlp(hs, nmlp, w_up_b, w_down_b, l, tm_s)
        hs = _ple(hs, nple, ps, w_pg_b, w_pp_b, gfin, l, tm_s, last)
    y_prompt = hp.reshape(bp, tp, d)
    y_sample = hs.reshape(bs, ts, d)
    return (y_prompt, y_sample, jnp.stack(kp_l), jnp.stack(vp_l), jnp.stack(gp_l),
            jnp.stack(ks_l), jnp.stack(vs_l), jnp.stack(gs_l))
```

```python
import functools

import jax
import jax.numpy as jnp
import numpy as np
from jax import lax
from jax.experimental import pallas as pl
from jax.experimental.pallas import tpu as pltpu

F32 = jnp.float32
BF16 = jnp.bfloat16

EPS = 1e-6
NEG_INF = -1e30
GLA_HEADS = 4
GLA_DK = 128
GLA_DV = 256
GLA_TAU = 16.0
GLA_CHUNK = 64
ATT_HEADS = 8
ATT_DH = 128
DILATED_BRANCHES = ((128, 1), (512, 4), (2048, 16))
MAX_WINDOW = 2048
GATE_RANK = 16

LANES = 128
VMEM_LIMIT = 56 * 1024 * 1024

Z_COLS = 6144
GLA_PART = 3072
ZC_QG, ZC_KG, ZC_VG, ZC_RG, ZC_QA, ZC_KA, ZC_VA = 0, 512, 1024, 2048, 3072, 4096, 5120


def _params(sem, vmem=VMEM_LIMIT):
    return pltpu.CompilerParams(dimension_semantics=sem, vmem_limit_bytes=vmem)


def _rms(x, g):
    return x * lax.rsqrt(jnp.mean(x * x, axis=-1, keepdims=True) + EPS) * g


def _dot(a, b):
    return jnp.dot(a.astype(BF16), b.astype(BF16), preferred_element_type=F32)


def _dot_nt(a, b):
    return lax.dot_general(a.astype(BF16), b.astype(BF16), (((1,), (1,)), ((), ())),
                           preferred_element_type=F32)


def _split2(x):
    hi = x.astype(BF16)
    lo = (x - hi.astype(F32)).astype(BF16)
    return hi, lo


def _dot_hi(a, b):
    ah, al = _split2(a)
    bh, bl = _split2(b)
    d = lambda u, v: jnp.dot(u, v, preferred_element_type=F32)
    return d(ah, bh) + d(ah, bl) + d(al, bh)


def _dot_exact(m, x):
    hi = x.astype(BF16)
    r1 = x - hi.astype(F32)
    mid = r1.astype(BF16)
    lo = (r1 - mid.astype(F32)).astype(BF16)
    d = lambda v: jnp.dot(m, v, preferred_element_type=F32)
    return d(hi) + d(mid) + d(lo)


def _log_sigmoid(x):
    return jnp.minimum(x, 0.0) - jnp.log1p(jnp.exp(-jnp.abs(x)))


def _silu(x):
    return x * jax.nn.sigmoid(x)


def _inproj_kernel(x_ref, g_ref, wa_ref, wb_ref, wl_ref, z_ref, glr_ref, xn_ref, *, nja):
    j = pl.program_id(1)

    @pl.when(j == 0)
    def _():
        xb = _rms(x_ref[...], g_ref[...]).astype(BF16)
        xn_ref[...] = xb
        glr_ref[...] = _dot_nt(xb, wl_ref[...])

    @pl.when(j < nja)
    def _():
        z_ref[...] = _dot_nt(xn_ref[...], wa_ref[...])

    @pl.when(j >= nja)
    def _():
        z_ref[...] = _dot_nt(xn_ref[...], wb_ref[...])


def _inproj(x, g, w_t, w_att, w_lr, l, tm, tn=1024):
    m, d = x.shape
    assert GLA_PART % tn == 0 and Z_COLS % tn == 0
    nja = GLA_PART // tn
    return pl.pallas_call(
        functools.partial(_inproj_kernel, nja=nja),
        grid=(m // tm, Z_COLS // tn),
        in_specs=[
            pl.BlockSpec((tm, d), lambda i, j: (i, 0)),
            pl.BlockSpec((None, 1, d), lambda i, j: (l, 0, 0)),
            pl.BlockSpec((None, tn, d), lambda i, j: (l, jnp.minimum(j, nja - 1), 0)),
            pl.BlockSpec((None, tn, d), lambda i, j: (l, jnp.maximum(j - nja, 0), 0)),
            pl.BlockSpec((None, LANES, d), lambda i, j: (l, 0, 0)),
        ],
        out_specs=[
            pl.BlockSpec((tm, tn), lambda i, j: (i, j)),
            pl.BlockSpec((tm, LANES), lambda i, j: (i, 0)),
        ],
        out_shape=[jax.ShapeDtypeStruct((m, Z_COLS), F32),
                   jax.ShapeDtypeStruct((m, LANES), F32)],
        scratch_shapes=[pltpu.VMEM((tm, d), BF16)],
        compiler_params=_params(("arbitrary", "arbitrary")),
        name="inproj",
    )(x, g, w_t, w_att, w_lr)


def _gla_prompt_kernel(q_ref, k_ref, v_ref, r_ref, glr_ref, w2_ref, bg_ref, gn_ref,
                       og_ref, sout_ref, s_ref, *, tb, chunk):
    step = pl.program_id(1)

    @pl.when(step == 0)
    def _():
        s_ref[...] = jnp.zeros_like(s_ref)

    c = chunk
    ri = lax.broadcasted_iota(jnp.int32, (c, c), 0)
    ci = lax.broadcasted_iota(jnp.int32, (c, c), 1)
    causal = ci <= ri
    csh = int(np.log2(c))
    ri2 = lax.broadcasted_iota(jnp.int32, (LANES, LANES), 0)
    ci2 = lax.broadcasted_iota(jnp.int32, (LANES, LANES), 1)
    same_chunk = (ri2 >> csh) == (ci2 >> csh)
    same2 = same_chunk.astype(BF16)
    tri2 = (same_chunk & (ci2 <= ri2)).astype(BF16)
    w2 = w2_ref[...]
    bg = bg_ref[...]
    gn = gn_ref[...]
    zpad_k = jnp.zeros((LANES - c, GLA_DK), F32)
    zpad_v = jnp.zeros((LANES - c, GLA_DV), BF16)

    g_all = _log_sigmoid(_dot_hi(glr_ref[...], w2) + bg) * (1.0 / GLA_TAU)
    b_parts, bl_parts = [], []
    for i in range(tb // LANES):
        g_pair = g_all[i * LANES:(i + 1) * LANES, :]
        b_parts.append(_dot_exact(tri2, g_pair))
        bl_parts.append(_dot_exact(same2, g_pair))
    b_all = jnp.concatenate(b_parts, axis=0)
    bl_all = jnp.concatenate(bl_parts, axis=0)
    e_q = jnp.exp(b_all)
    e_k = jnp.exp(-b_all)
    e_end = jnp.exp(bl_all - b_all)
    e_dec = jnp.exp(bl_all)
    state = [s_ref[h] for h in range(GLA_HEADS)]
    for n in range(tb // c):
        rows = slice(n * c, (n + 1) * c)
        for h in range(GLA_HEADS):
            ks = slice(h * GLA_DK, (h + 1) * GLA_DK)
            vs = slice(h * GLA_DV, (h + 1) * GLA_DV)
            q = q_ref[rows, ks] * (GLA_DK ** -0.5)
            k = k_ref[rows, ks]
            v = v_ref[rows, vs].astype(BF16)
            qd = (q * e_q[rows, ks]).astype(BF16)
            kd = k * e_k[rows, ks]
            ke = k * e_end[rows, ks]
            a = jnp.where(causal, _dot_nt(qd, kd), 0.0)
            s_old = state[h]
            o = _dot(a, v) + _dot(qd, s_old)
            ke_t = jnp.concatenate([ke, zpad_k], axis=0).T.astype(BF16)
            v_pad = jnp.concatenate([v, zpad_v], axis=0)
            ds = jnp.dot(ke_t, v_pad, preferred_element_type=F32)
            dec = jnp.broadcast_to(e_dec[n * c:n * c + 1, ks], (GLA_DK, GLA_DK)).T
            state[h] = jnp.concatenate([dec, dec], axis=1) * s_old + ds
            on = _rms(o, gn)
            og_ref[rows, vs] = (on * _silu(r_ref[rows, vs])).astype(og_ref.dtype)
    for h in range(GLA_HEADS):
        s_ref[h] = state[h]

    @pl.when(step == pl.num_programs(1) - 1)
    def _():
        sout_ref[...] = s_ref[...]


def _gla_prompt(z3, glr3, w2p, bg, gn, l, tb=512):
    b, t, _ = z3.shape
    chunk = np.gcd(t, GLA_CHUNK)
    assert t % tb == 0 and tb % LANES == 0 and LANES % chunk == 0 and chunk & (chunk - 1) == 0
    qk_w = GLA_HEADS * GLA_DK
    v_w = GLA_HEADS * GLA_DV
    return pl.pallas_call(
        functools.partial(_gla_prompt_kernel, tb=tb, chunk=int(chunk)),
        grid=(b, t // tb),
        in_specs=[
            pl.BlockSpec((None, tb, qk_w), lambda i, s: (i, s, ZC_QG // qk_w)),
            pl.BlockSpec((None, tb, qk_w), lambda i, s: (i, s, ZC_KG // qk_w)),
            pl.BlockSpec((None, tb, v_w), lambda i, s: (i, s, ZC_VG // v_w)),
            pl.BlockSpec((None, tb, v_w), lambda i, s: (i, s, ZC_RG // v_w)),
            pl.BlockSpec((None, tb, LANES), lambda i, s: (i, s, 0)),
            pl.BlockSpec((None, LANES, qk_w), lambda i, s: (l, 0, 0)),
            pl.BlockSpec((None, 1, qk_w), lambda i, s: (l, 0, 0)),
            pl.BlockSpec((None, 1, GLA_DV), lambda i, s: (l, 0, 0)),
        ],
        out_specs=[
            pl.BlockSpec((None, tb, v_w), lambda i, s: (i, s, 0)),
            pl.BlockSpec((None, GLA_HEADS, GLA_DK, GLA_DV), lambda i, s: (i, 0, 0, 0)),
        ],
        out_shape=[jax.ShapeDtypeStruct((b, t, v_w), BF16),
                   jax.ShapeDtypeStruct((b, GLA_HEADS, GLA_DK, GLA_DV), F32)],
        scratch_shapes=[pltpu.VMEM((GLA_HEADS, GLA_DK, GLA_DV), F32)],
        compiler_params=_params(("arbitrary", "arbitrary")),
        name="gla_prompt",
    )(z3, z3, z3, z3, glr3, w2p, bg, gn)


def _gla_sample_kernel(q_ref, k_ref, v_ref, r_ref, glr_ref, w2_ref, bg_ref, gn_ref, s0_ref,
                       *rest, nb, tq, layer):
    og_ref, sn_ref = rest[-2:]
    rws = nb * tq
    sh = int(np.log2(tq))
    ri = lax.broadcasted_iota(jnp.int32, (rws, rws), 0)
    ci = lax.broadcasted_iota(jnp.int32, (rws, rws), 1)
    same = (ri >> sh) == (ci >> sh)
    causal = same & (ci <= ri)
    x = _dot_hi(glr_ref[...], w2_ref[...]) + bg_ref[...]
    g = _log_sigmoid(x) * (1.0 / GLA_TAU)
    b = _dot_exact(causal.astype(BF16), g)
    bl = _dot_exact(same.astype(BF16), g)
    q = q_ref[...] * (GLA_DK ** -0.5)
    k = k_ref[...]
    qd = (q * jnp.exp(b)).astype(BF16)
    kd = k * jnp.exp(-b)
    ke = k * jnp.exp(bl - b)
    v = v_ref[...].astype(BF16)
    a = jnp.where(causal, _dot_nt(qd, kd), 0.0)
    o = _dot(a, v)
    ke_t = ke.T
    dec_t = jnp.exp(bl).T
    row_seq = lax.broadcasted_iota(jnp.int32, (rws, GLA_DV), 0) >> sh
    col_seq = lax.broadcasted_iota(jnp.int32, (GLA_DK, rws), 1) >> sh
    for s in range(nb):
        s0 = s0_ref[s]
        o = o + jnp.where(row_seq == s, _dot(qd, s0), 0.0)
        lhs = jnp.where(col_seq == s, ke_t, 0.0).astype(BF16)
        ds = jnp.dot(lhs, v, preferred_element_type=F32)
        s_new = dec_t[:, s * tq:s * tq + 1] * s0 + ds
        if layer is None:
            sn_ref[s] = s_new
        else:
            sn_ref[layer, s] = s_new
    if layer is not None:
        for other in range(sn_ref.shape[0]):
            if other != layer:
                sn_ref[other] = jnp.zeros(sn_ref.shape[1:], sn_ref.dtype)
    og_ref[...] = (_rms(o, gn_ref[...]) * _silu(r_ref[...])).astype(og_ref.dtype)


def _gla_sample(z, glr, w2p, bg, gn, state, stacked, l, nseq, tq, nb=32):
    assert nb * tq == LANES and nseq % nb == 0
    m = z.shape[0]
    depth = state.shape[0]
    prev = () if stacked is None else (stacked,)
    return pl.pallas_call(
        functools.partial(_gla_sample_kernel, nb=nb, tq=tq, layer=None if prev else l),
        grid=(nseq // nb, GLA_HEADS),
        in_specs=[
            pl.BlockSpec((LANES, GLA_DK), lambda i, h: (i, ZC_QG // GLA_DK + h)),
            pl.BlockSpec((LANES, GLA_DK), lambda i, h: (i, ZC_KG // GLA_DK + h)),
            pl.BlockSpec((LANES, GLA_DV), lambda i, h: (i, ZC_VG // GLA_DV + h)),
            pl.BlockSpec((LANES, GLA_DV), lambda i, h: (i, ZC_RG // GLA_DV + h)),
            pl.BlockSpec((LANES, LANES), lambda i, h: (i, 0)),
            pl.BlockSpec((None, LANES, GLA_DK), lambda i, h: (l, 0, h)),
            pl.BlockSpec((None, 1, GLA_DK), lambda i, h: (l, 0, h)),
            pl.BlockSpec((None, 1, GLA_DV), lambda i, h: (l, 0, 0)),
            pl.BlockSpec((None, nb, None, GLA_DK, GLA_DV), lambda i, h: (l, i, h, 0, 0)),
        ] + [pl.BlockSpec(memory_space=pl.ANY)] * len(prev),
        out_specs=[
            pl.BlockSpec((LANES, GLA_DV), lambda i, h: (i, h)),
            (pl.BlockSpec((None, nb, None, GLA_DK, GLA_DV), lambda i, h: (l, i, h, 0, 0)) if prev else
             pl.BlockSpec((depth, nb, None, GLA_DK, GLA_DV), lambda i, h: (0, i, h, 0, 0))),
        ],
        out_shape=[jax.ShapeDtypeStruct((m, GLA_HEADS * GLA_DV), BF16),
                   jax.ShapeDtypeStruct((depth, nseq, GLA_HEADS, GLA_DK, GLA_DV), F32)],
        input_output_aliases={9: 1} if prev else {},
        compiler_params=_params(("arbitrary", "arbitrary")),
        name="gla_sample",
    )(z, z, z, z, glr, w2p, bg, gn, state, *prev)


def _prompt_attention_blocks(slope, q_ref, k_ref, v_ref, accs, ms, ls, sub, *, t, branches, per_sub):
    n = LANES
    scale = ATT_DH ** -0.5
    qi = lax.broadcasted_iota(jnp.int32, (n, 2 * n), 0)
    kc = lax.broadcasted_iota(jnp.int32, (n, 2 * n), 1)
    delta = n + qi - kc
    band = (delta >= 0) & (delta <= n)
    own = kc >= n
    ones = jnp.ones((2 * n, n), BF16)

    for bi, (w, d) in enumerate(branches):
        nblk = t // (n * d)
        assert w // d == n and t % (n * d) == 0 and nblk & (nblk - 1) == 0
        bias = delta.astype(F32) * (slope * d)
        for u in range(per_sub):
            idx = sub * per_sub + u
            r = idx >> int(np.log2(nblk))
            i = idx & (nblk - 1)
            start = r + i * (n * d)
            pstart = r + jnp.maximum(i - 1, 0) * (n * d)
            if d == 1:
                rows, prows = pl.ds(start, n), pl.ds(pstart, n)
            else:
                rows, prows = pl.ds(start, n, stride=d), pl.ds(pstart, n, stride=d)
            q = q_ref[rows, :].astype(BF16)
            k2 = jnp.concatenate([k_ref[prows, :].astype(BF16), k_ref[rows, :].astype(BF16)], axis=0)
            v2 = jnp.concatenate([v_ref[prows, :].astype(BF16), v_ref[rows, :].astype(BF16)], axis=0)
            valid = band & (own | (i > 0))
            s = jnp.where(valid, _dot_nt(q, k2) * scale - bias, NEG_INF)
            m_b = jnp.max(s, axis=-1, keepdims=True)
            p = jnp.exp(s - m_b).astype(BF16)
            ol = jnp.dot(p, jnp.concatenate([v2, ones], axis=1),
                         preferred_element_type=F32)
            accs[bi][rows, :] = ol[:, :n]
            ms[bi][rows, :] = jnp.broadcast_to(m_b, (n, n))
            ls[bi][rows, :] = ol[:, n:]


def _prompt_attention_combine(o_ref, accs, ms, ls, *, t, rows_per_step=256):
    def fin(c, carry):
        rows = pl.ds(pl.multiple_of(c * rows_per_step, rows_per_step), rows_per_step)
        m_parts = [m[rows, :] for m in ms]
        m_all = functools.reduce(jnp.maximum, m_parts)
        num = jnp.zeros((rows_per_step, LANES), F32)
        den = jnp.zeros((rows_per_step, LANES), F32)
        for acc, m_b, l in zip(accs, m_parts, ls):
            a = jnp.exp(m_b - m_all)
            num = num + a * acc[rows, :]
            den = den + a * l[rows, :]
        o_ref[rows, :] = (num / den).astype(o_ref.dtype)
        return carry

    lax.fori_loop(0, t // rows_per_step, fin, 0)


def _sample_attention(sl_ref, q_ref, kn_ref, vn_ref, khi_ref, klo_ref, vhi_ref, vlo_ref,
                      o_ref, cnt_ref, bias_ref, first_step, *, tq, wbuf, hi0, branches):
    nh = ATT_HEADS
    hsh = int(np.log2(nh))
    rq = tq * nh
    cp = LANES
    scale = ATT_DH ** -0.5
    slope = sl_ref[:, :1]
    n_hi = khi_ref.shape[0] // cp
    g = max(d for _, d in branches)
    nu = klo_ref.shape[1]
    mper = cp // nu
    ush = int(np.log2(nu))
    n_lo = klo_ref.shape[0] // mper

    def iotas(width):
        row = lax.broadcasted_iota(jnp.int32, (rq, width), 0)
        lane = lax.broadcasted_iota(jnp.int32, (rq, width), 1)
        return row >> hsh, (row & (nh - 1)) == (lane & (nh - 1)), lane >> hsh

    def multiplicity(dist, head_ok):
        cnt = jnp.zeros(dist.shape, F32)
        for (w, d) in branches:
            cnt = cnt + ((dist & (d - 1)) == 0).astype(F32) * (dist <= w).astype(F32)
        return jnp.where((dist >= 0) & head_ok, cnt, 0.0)

    @pl.when(first_step)
    def _():
        t_row, head_ok, pos_in = iotas(cp * nh)
        for c in range(n_hi):
            dist = wbuf + t_row - (hi0 + c * cp + pos_in)
            cnt_ref[c] = multiplicity(dist, head_ok)
            bias_ref[c] = slope * dist.astype(F32)
        for c in range(n_lo):
            pos = g * (c * mper + (pos_in >> ush)) + (pos_in & (nu - 1))
            dist = wbuf + t_row - pos
            cnt_ref[n_hi + c] = multiplicity(dist, head_ok)
            bias_ref[n_hi + c] = slope * dist.astype(F32)

    q2 = q_ref[...].reshape(rq, ATT_DH).astype(BF16)

    def slab(ref, c, per):
        return ref[c * per:(c + 1) * per].reshape(cp * nh, ATT_DH)

    def score(kc, cnt, bias):
        return jnp.where(cnt > 0.0, _dot_nt(q2, kc) * scale - bias, NEG_INF)

    keys = [slab(khi_ref, c, cp) for c in range(n_hi)] + [slab(klo_ref, c, mper) for c in range(n_lo)]
    vals = [slab(vhi_ref, c, cp) for c in range(n_hi)] + [slab(vlo_ref, c, mper) for c in range(n_lo)]
    scores = [score(kc, cnt_ref[c], bias_ref[c]) for c, kc in enumerate(keys)]
    t_row, head_ok, pos_in = iotas(rq)
    dist = t_row - pos_in
    cnt_new = multiplicity(dist, head_ok)
    s_new = score(kn_ref[...].reshape(rq, ATT_DH), cnt_new, slope * dist.astype(F32))
    m_all = scores[0]
    for s in scores[1:]:
        m_all = jnp.maximum(m_all, s)
    m = jnp.maximum(jnp.max(m_all, axis=-1, keepdims=True), jnp.max(s_new, axis=-1, keepdims=True))
    p_new = cnt_new * jnp.exp(s_new - m)
    acc = _dot(p_new, vn_ref[...].reshape(rq, ATT_DH))
    psum = jnp.zeros((rq, cp * nh), F32)
    for c, (s, vc) in enumerate(zip(scores, vals)):
        p = cnt_ref[c] * jnp.exp(s - m)
        psum = psum + p
        acc = acc + _dot(p, vc)
    l = jnp.sum(psum, axis=-1, keepdims=True) + jnp.sum(p_new, axis=-1, keepdims=True)
    o_ref[...] = (acc / l).reshape(tq, nh, ATT_DH)


def _attn_kernel(slopes_ref, q_ref, k_ref, v_ref, sl_ref, qs_ref, kn_ref, vn_ref,
                 khi_ref, klo_ref, vhi_ref, vlo_ref, o_ref, os_ref, *scratch,
                 t, branches, nsub, tq, wbuf, hi0):
    accs, ms, ls = scratch[0:3], scratch[3:6], scratch[6:9]
    cnt_ref, bias_ref = scratch[9:11]
    sub = pl.program_id(2)
    first_step = (pl.program_id(0) == 0) & (pl.program_id(1) == 0) & (sub == 0)
    _sample_attention(sl_ref, qs_ref, kn_ref, vn_ref, khi_ref, klo_ref, vhi_ref, vlo_ref,
                      os_ref, cnt_ref, bias_ref, first_step,
                      tq=tq, wbuf=wbuf, hi0=hi0, branches=branches)
    _prompt_attention_blocks(slopes_ref[pl.program_id(1)], q_ref, k_ref, v_ref, accs, ms, ls, sub,
                             t=t, branches=branches, per_sub=t // LANES // nsub)

    @pl.when(sub == nsub - 1)
    def _():
        _prompt_attention_combine(o_ref, accs, ms, ls, t=t)


def _attn(slopes, slope_rows, z3, q, kn, vn, cache_k, cache_v, l):
    b, t, _ = z3.shape
    nseq, tq, nh, dh = q.shape
    depth, _, wbuf, _, _ = cache_k.shape
    nsub = nseq // (b * nh)
    assert nsub * b * nh == nseq and (t // LANES) % nsub == 0
    g = max(d for _, d in DILATED_BRANCHES)
    w_dense = max(w for w, d in DILATED_BRANCHES if d < g)
    assert nh == ATT_HEADS and dh == ATT_DH and tq <= g and LANES % tq == 0
    assert all(d & (d - 1) == 0 for _, d in DILATED_BRANCHES)
    assert wbuf % w_dense == 0 and wbuf % g == 0 and w_dense % LANES == 0
    assert wbuf >= max(w for w, _ in DILATED_BRANCHES)
    hi0 = wbuf - w_dense
    n_lo = hi0 // g
    assert (n_lo * tq) % LANES == 0
    n_chunks = w_dense // LANES + n_lo * tq // LANES
    ck = cache_k.reshape(depth, nseq, wbuf // g, g, nh, dh)
    cv = cache_v.reshape(depth, nseq, wbuf // g, g, nh, dh)
    seq = lambda i, h, s: (i * nh + h) * nsub + s
    prm = lambda c0: pl.BlockSpec((None, t, dh), lambda i, h, s, sl: (i, 0, c0 // dh + h))
    new = pl.BlockSpec((None, tq, nh, dh), lambda i, h, s, sl: (seq(i, h, s), 0, 0, 0))
    hi = pl.BlockSpec((None, None, w_dense, nh, dh),
                      lambda i, h, s, sl: (l, seq(i, h, s), wbuf // w_dense - 1, 0, 0))
    lo = pl.BlockSpec((None, None, n_lo, tq, nh, dh), lambda i, h, s, sl: (l, seq(i, h, s), 0, 0, 0, 0))
    return pl.pallas_call(
        functools.partial(_attn_kernel, t=t, branches=DILATED_BRANCHES, nsub=nsub,
                          tq=tq, wbuf=wbuf, hi0=hi0),
        grid_spec=pltpu.PrefetchScalarGridSpec(
            num_scalar_prefetch=1,
            grid=(b, nh, nsub),
            in_specs=[prm(ZC_QA), prm(ZC_KA), prm(ZC_VA),
                      pl.BlockSpec((tq * nh, LANES), lambda i, h, s, sl: (0, 0)),
                      new, new, new, hi, lo, hi, lo],
            out_specs=[pl.BlockSpec((None, t, dh), lambda i, h, s, sl: (i, 0, h)), new],
            scratch_shapes=([pltpu.VMEM((t, dh), F32)] * 9
                            + [pltpu.VMEM((n_chunks, tq * nh, LANES * nh), F32)] * 2),
        ),
        out_shape=[jax.ShapeDtypeStruct((b, t, nh * dh), BF16),
                   jax.ShapeDtypeStruct((nseq, tq, nh, dh), F32)],
        compiler_params=_params(("arbitrary", "arbitrary", "arbitrary")),
        name="attn",
    )(slopes, z3, z3, z3, slope_rows, q, kn, vn, cache_k, ck, cache_v, cv)


def _outproj_kernel(x_ref, og_ref, oa_ref, wt_ref, wb_ref, o_ref):
    o_ref[...] = (x_ref[...]
                  + jnp.dot(og_ref[...], wt_ref[...], preferred_element_type=F32)
                  + jnp.dot(oa_ref[...], wb_ref[...], preferred_element_type=F32))


def _outproj(x, og, oa, w_out, l, tm, tn=512):
    m, d = x.shape
    half = og.shape[1]
    return pl.pallas_call(
        _outproj_kernel,
        grid=(m // tm, d // tn),
        in_specs=[
            pl.BlockSpec((tm, tn), lambda i, j: (i, j)),
            pl.BlockSpec((tm, half), lambda i, j: (i, 0)),
            pl.BlockSpec((tm, half), lambda i, j: (i, 0)),
            pl.BlockSpec((None, half, tn), lambda i, j: (l, 0, j)),
            pl.BlockSpec((None, half, tn), lambda i, j: (l, 1, j)),
        ],
        out_specs=pl.BlockSpec((tm, tn), lambda i, j: (i, j)),
        out_shape=jax.ShapeDtypeStruct((m, d), F32),
        compiler_params=_params(("arbitrary", "arbitrary")),
        name="outproj",
    )(x, og, oa, w_out, w_out)


def _mlp_kernel(x_ref, g_ref, wu_ref, wd_ref, o_ref, xn_ref, h_ref, *, na, tf, tn):
    j = pl.program_id(1)

    @pl.when(j == 0)
    def _():
        xn_ref[...] = _rms(x_ref[...], g_ref[...]).astype(BF16)

    @pl.when(j < na)
    def _():
        u = jnp.dot(xn_ref[...], wu_ref[...], preferred_element_type=F32)
        h_ref[j] = jnp.square(jnp.maximum(u, 0.0)).astype(BF16)

    @pl.when(j >= na)
    def _():
        cols = pl.ds(pl.multiple_of((j - na) * tn, tn), tn)
        acc = x_ref[:, cols]
        for c in range(na):
            acc = acc + jnp.dot(h_ref[c], wd_ref[c * tf:(c + 1) * tf, :], preferred_element_type=F32)
        o_ref[...] = acc


def _mlp(x, g, w_up, w_down, l, tm, tf=512, tn=512):
    m, d = x.shape
    ff = w_up.shape[2]
    na, nb = ff // tf, d // tn
    return pl.pallas_call(
        functools.partial(_mlp_kernel, na=na, tf=tf, tn=tn),
        grid=(m // tm, na + nb),
        in_specs=[
            pl.BlockSpec((tm, d), lambda i, j: (i, 0), pipeline_mode=pl.Buffered(1)),
            pl.BlockSpec((None, 1, d), lambda i, j: (l, 0, 0)),
            pl.BlockSpec((None, d, tf), lambda i, j: (l, 0, jnp.minimum(j, na - 1))),
            pl.BlockSpec((None, ff, tn), lambda i, j: (l, 0, jnp.maximum(j - na, 0))),
        ],
        out_specs=pl.BlockSpec((tm, tn), lambda i, j: (i, jnp.maximum(j - na, 0))),
        out_shape=jax.ShapeDtypeStruct((m, d), F32),
        scratch_shapes=[pltpu.VMEM((tm, d), BF16), pltpu.VMEM((na, tm, tf), BF16)],
        compiler_params=_params(("arbitrary", "arbitrary")),
        name="mlp",
    )(x, g, w_up, w_down)


def _ple_kernel(x_ref, g_ref, p_ref, wg_ref, wp_ref, gf_ref, o_ref, xn_ref, *, tn, final):
    j = pl.program_id(1)

    @pl.when(j == 0)
    def _():
        xn_ref[...] = _rms(x_ref[...], g_ref[...]).astype(BF16)

    cols = pl.ds(pl.multiple_of(j * tn, tn), tn)
    gate = jax.nn.sigmoid(jnp.dot(xn_ref[...], wg_ref[...], preferred_element_type=F32))
    o_ref[:, cols] = x_ref[:, cols] + gate * _dot(p_ref[...], wp_ref[...])

    if final:
        @pl.when(j == pl.num_programs(1) - 1)
        def _():
            o_ref[...] = _rms(o_ref[...], gf_ref[...])


def _ple(x, g, p, w_gate, w_proj, g_final, l, tm, final, tn=512):
    m, d = x.shape
    pd = p.shape[-1]
    return pl.pallas_call(
        functools.partial(_ple_kernel, tn=tn, final=final),
        grid=(m // tm, d // tn),
        in_specs=[
            pl.BlockSpec((tm, d), lambda i, j: (i, 0)),
            pl.BlockSpec((None, 1, d), lambda i, j: (l, 0, 0)),
            pl.BlockSpec((None, tm, pd), lambda i, j: (l, i, 0)),
            pl.BlockSpec((None, d, tn), lambda i, j: (l, 0, j)),
            pl.BlockSpec((None, pd, tn), lambda i, j: (l, 0, j)),
            pl.BlockSpec((1, d), lambda i, j: (0, 0)),
        ],
        out_specs=pl.BlockSpec((tm, d), lambda i, j: (i, 0)),
        out_shape=jax.ShapeDtypeStruct((m, d), F32),
        scratch_shapes=[pltpu.VMEM((tm, d), BF16)],
        compiler_params=_params(("arbitrary", "arbitrary")),
        name="ple",
    )(x, g, p, w_gate, w_proj, g_final)


def kernel(x_prompt, x_sample, cache_k_win, cache_v_win, state_gla, p_prompt, p_sample,
           norm_mix, w_in, w_gate2, b_gate, gla_norm, w_out, norm_mlp, w_up, w_down,
           norm_ple, w_ple_gate, w_ple_proj, norm_final):
    bp, tp, d = x_prompt.shape
    bs, ts, _ = x_sample.shape
    depth = w_in.shape[0]
    mp, ms = bp * tp, bs * ts
    tm_p, tm_s = 1024, ms
    w_keep = min(MAX_WINDOW, tp)

    slopes_np = (2.0 ** (-8.0 * np.arange(1, ATT_HEADS + 1) / ATT_HEADS)).astype(np.float32)
    slopes = jnp.asarray(slopes_np)
    slope_rows = jnp.asarray(np.broadcast_to(np.tile(slopes_np, ts)[:, None], (ts * ATT_HEADS, LANES)))
    lr0 = GLA_PART
    w_z = jnp.swapaxes(w_in, 1, 2).astype(BF16)
    w_att = w_z[:, lr0 + GATE_RANK:]
    w_lr = jnp.pad(w_z[:, lr0:lr0 + GATE_RANK], ((0, 0), (0, LANES - GATE_RANK), (0, 0)))
    w2p = jnp.pad(w_gate2, ((0, 0), (0, LANES - GATE_RANK), (0, 0)))
    w_out_b, w_up_b, w_down_b = w_out.astype(BF16), w_up.astype(BF16), w_down.astype(BF16)
    w_pg_b, w_pp_b = w_ple_gate.astype(BF16), w_ple_proj.astype(BF16)
    row = lambda a: a.reshape(depth, 1, a.shape[-1])
    nmix, nmlp, nple, bg, gn = row(norm_mix), row(norm_mlp), row(norm_ple), row(b_gate), row(gla_norm)
    gfin = norm_final.reshape(1, d)
    pp = p_prompt.reshape(depth, mp, -1)
    ps = p_sample.reshape(depth, ms, -1)

    hp = x_prompt.reshape(mp, d)
    hs = x_sample.reshape(ms, d)
    kp_l, vp_l, gp_l, ks_l, vs_l = [], [], [], [], []
    gs = None
    for l in range(depth):
        last = l == depth - 1
        z, glr = _inproj(hp, nmix, w_z, w_att, w_lr, l, tm_p)
        z3 = z.reshape(bp, tp, Z_COLS)
        og, s_new = _gla_prompt(z3, glr.reshape(bp, tp, LANES), w2p, bg, gn, l)
        zs, glr_s = _inproj(hs, nmix, w_z, w_att, w_lr, l, tm_s)
        og_s, gs = _gla_sample(zs, glr_s, w2p, bg, gn, state_gla, gs, l, bs, ts)
        heads = lambda c0: zs[:, c0:c0 + ATT_HEADS * ATT_DH].reshape(bs, ts, ATT_HEADS, ATT_DH)
        q_s, k_s, v_s = heads(ZC_QA), heads(ZC_KA), heads(ZC_VA)
        oa, oa_s = _attn(slopes, slope_rows, z3, q_s, k_s, v_s, cache_k_win, cache_v_win, l)
        kp_l.append(z3[:, tp - w_keep:, ZC_KA:ZC_VA].reshape(bp, w_keep, ATT_HEADS, ATT_DH))
        vp_l.append(z3[:, tp - w_keep:, ZC_VA:].reshape(bp, w_keep, ATT_HEADS, ATT_DH))
        gp_l.append(s_new)
        ks_l.append(k_s)
        vs_l.append(v_s)
        hp = _outproj(hp, og.reshape(mp, -1), oa.reshape(mp, -1), w_out_b, l, tm_p)
        hp = _mlp(hp, nmlp, w_up_b, w_down_b, l, tm_p)
        hp = _ple(hp, nple, pp, w_pg_b, w_pp_b, gfin, l, tm_p, last)
        hs = _outproj(hs, og_s, oa_s.reshape(ms, -1).astype(BF16), w_out_b, l, tm_s)
        hs = _mlp(hs, nmlp, w_up_b, w_down_b, l, tm_s)
        hs = _ple(hs, nple, ps, w_pg_b, w_pp_b, gfin, l, tm_s, last)
    y_prompt = hp.reshape(bp, tp, d)
    y_sample = hs.reshape(bs, ts, d)
    return (y_prompt, y_sample, jnp.stack(kp_l), jnp.stack(vp_l), jnp.stack(gp_l),
            jnp.stack(ks_l), jnp.stack(vs_l), gs)
```

```python
import functools

import jax
import jax.numpy as jnp
import numpy as np
from jax import lax
from jax.experimental import pallas as pl
from jax.experimental.pallas import tpu as pltpu

F32 = jnp.float32
BF16 = jnp.bfloat16

EPS = 1e-6
NEG_INF = -1e30
GLA_HEADS = 4
GLA_DK = 128
GLA_DV = 256
GLA_TAU = 16.0
GLA_CHUNK = 64
ATT_HEADS = 8
ATT_DH = 128
DILATED_BRANCHES = ((128, 1), (512, 4), (2048, 16))
MAX_WINDOW = 2048
GATE_RANK = 16

LANES = 128
VMEM_LIMIT = 56 * 1024 * 1024

Z_COLS = 6144
GLA_PART = 3072
ZC_QG, ZC_KG, ZC_VG, ZC_RG, ZC_QA, ZC_KA, ZC_VA = 0, 512, 1024, 2048, 3072, 4096, 5120


def _params(sem, vmem=VMEM_LIMIT):
    return pltpu.CompilerParams(dimension_semantics=sem, vmem_limit_bytes=vmem)


def _rms(x, g):
    return x * lax.rsqrt(jnp.mean(x * x, axis=-1, keepdims=True) + EPS) * g


def _dot(a, b):
    return jnp.dot(a.astype(BF16), b.astype(BF16), preferred_element_type=F32)


def _dot_nt(a, b):
    return lax.dot_general(a.astype(BF16), b.astype(BF16), (((1,), (1,)), ((), ())),
                           preferred_element_type=F32)


def _split2(x):
    hi = x.astype(BF16)
    lo = (x - hi.astype(F32)).astype(BF16)
    return hi, lo


def _dot_hi(a, b):
    ah, al = _split2(a)
    bh, bl = _split2(b)
    d = lambda u, v: jnp.dot(u, v, preferred_element_type=F32)
    return d(ah, bh) + d(ah, bl) + d(al, bh)


def _dot_exact(m, x):
    hi = x.astype(BF16)
    r1 = x - hi.astype(F32)
    mid = r1.astype(BF16)
    lo = (r1 - mid.astype(F32)).astype(BF16)
    d = lambda v: jnp.dot(m, v, preferred_element_type=F32)
    return d(hi) + d(mid) + d(lo)


def _log_sigmoid(x):
    return jnp.minimum(x, 0.0) - jnp.log1p(jnp.exp(-jnp.abs(x)))


def _silu(x):
    return x * jax.nn.sigmoid(x)


def _inproj_kernel(x_ref, g_ref, wa_ref, wb_ref, wl_ref, z_ref, glr_ref, xn_ref, *, nja):
    j = pl.program_id(1)

    @pl.when(j == 0)
    def _():
        xb = _rms(x_ref[...], g_ref[...]).astype(BF16)
        xn_ref[...] = xb
        glr_ref[...] = _dot_nt(xb, wl_ref[...])

    @pl.when(j < nja)
    def _():
        z_ref[...] = _dot_nt(xn_ref[...], wa_ref[...])

    @pl.when(j >= nja)
    def _():
        z_ref[...] = _dot_nt(xn_ref[...], wb_ref[...])


def _inproj(x, g, w_t, w_att, w_lr, l, tm, tn=1024):
    m, d = x.shape
    assert GLA_PART % tn == 0 and Z_COLS % tn == 0
    nja = GLA_PART // tn
    return pl.pallas_call(
        functools.partial(_inproj_kernel, nja=nja),
        grid=(m // tm, Z_COLS // tn),
        in_specs=[
            pl.BlockSpec((tm, d), lambda i, j: (i, 0)),
            pl.BlockSpec((None, 1, d), lambda i, j: (l, 0, 0)),
            pl.BlockSpec((None, tn, d), lambda i, j: (l, jnp.minimum(j, nja - 1), 0)),
            pl.BlockSpec((None, tn, d), lambda i, j: (l, jnp.maximum(j - nja, 0), 0)),
            pl.BlockSpec((None, LANES, d), lambda i, j: (l, 0, 0)),
        ],
        out_specs=[
            pl.BlockSpec((tm, tn), lambda i, j: (i, j)),
            pl.BlockSpec((tm, LANES), lambda i, j: (i, 0)),
        ],
        out_shape=[jax.ShapeDtypeStruct((m, Z_COLS), F32),
                   jax.ShapeDtypeStruct((m, LANES), F32)],
        scratch_shapes=[pltpu.VMEM((tm, d), BF16)],
        compiler_params=_params(("arbitrary", "arbitrary")),
        name="inproj",
    )(x, g, w_t, w_att, w_lr)


def _gla_prompt_kernel(q_ref, k_ref, v_ref, r_ref, glr_ref, w2_ref, bg_ref, gn_ref,
                       og_ref, sout_ref, s_ref, *, tb, chunk):
    step = pl.program_id(1)

    @pl.when(step == 0)
    def _():
        s_ref[...] = jnp.zeros_like(s_ref)

    c = chunk
    ri = lax.broadcasted_iota(jnp.int32, (c, c), 0)
    ci = lax.broadcasted_iota(jnp.int32, (c, c), 1)
    causal = ci <= ri
    csh = int(np.log2(c))
    ri2 = lax.broadcasted_iota(jnp.int32, (LANES, LANES), 0)
    ci2 = lax.broadcasted_iota(jnp.int32, (LANES, LANES), 1)
    same_chunk = (ri2 >> csh) == (ci2 >> csh)
    same2 = same_chunk.astype(BF16)
    tri2 = (same_chunk & (ci2 <= ri2)).astype(BF16)
    w2 = w2_ref[...]
    bg = bg_ref[...]
    gn = gn_ref[...]
    zpad_k = jnp.zeros((LANES - c, GLA_DK), F32)
    zpad_v = jnp.zeros((LANES - c, GLA_DV), BF16)

    g_all = _log_sigmoid(_dot_hi(glr_ref[...], w2) + bg) * (1.0 / GLA_TAU)
    b_parts, bl_parts = [], []
    for i in range(tb // LANES):
        g_pair = g_all[i * LANES:(i + 1) * LANES, :]
        b_parts.append(_dot_exact(tri2, g_pair))
        bl_parts.append(_dot_exact(same2, g_pair))
    b_all = jnp.concatenate(b_parts, axis=0)
    bl_all = jnp.concatenate(bl_parts, axis=0)
    e_q = jnp.exp(b_all)
    e_k = jnp.exp(-b_all)
    e_end = jnp.exp(bl_all - b_all)
    e_dec = jnp.exp(bl_all)
    state = [s_ref[h] for h in range(GLA_HEADS)]
    for n in range(tb // c):
        rows = slice(n * c, (n + 1) * c)
        for h in range(GLA_HEADS):
            ks = slice(h * GLA_DK, (h + 1) * GLA_DK)
            vs = slice(h * GLA_DV, (h + 1) * GLA_DV)
            q = q_ref[rows, ks] * (GLA_DK ** -0.5)
            k = k_ref[rows, ks]
            v = v_ref[rows, vs].astype(BF16)
            qd = (q * e_q[rows, ks]).astype(BF16)
            kd = k * e_k[rows, ks]
            ke = k * e_end[rows, ks]
            a = jnp.where(causal, _dot_nt(qd, kd), 0.0)
            s_old = state[h]
            o = _dot(a, v) + _dot(qd, s_old)
            ke_t = jnp.concatenate([ke, zpad_k], axis=0).T.astype(BF16)
            v_pad = jnp.concatenate([v, zpad_v], axis=0)
            ds = jnp.dot(ke_t, v_pad, preferred_element_type=F32)
            dec = jnp.broadcast_to(e_dec[n * c:n * c + 1, ks], (GLA_DK, GLA_DK)).T
            state[h] = jnp.concatenate([dec, dec], axis=1) * s_old + ds
            on = _rms(o, gn)
            og_ref[rows, vs] = (on * _silu(r_ref[rows, vs])).astype(og_ref.dtype)
    for h in range(GLA_HEADS):
        s_ref[h] = state[h]

    @pl.when(step == pl.num_programs(1) - 1)
    def _():
        sout_ref[...] = s_ref[...]


def _gla_prompt(z3, glr3, w2p, bg, gn, l, tb=512):
    b, t, _ = z3.shape
    chunk = np.gcd(t, GLA_CHUNK)
    assert t % tb == 0 and tb % LANES == 0 and LANES % chunk == 0 and chunk & (chunk - 1) == 0
    qk_w = GLA_HEADS * GLA_DK
    v_w = GLA_HEADS * GLA_DV
    return pl.pallas_call(
        functools.partial(_gla_prompt_kernel, tb=tb, chunk=int(chunk)),
        grid=(b, t // tb),
        in_specs=[
            pl.BlockSpec((None, tb, qk_w), lambda i, s: (i, s, ZC_QG // qk_w)),
            pl.BlockSpec((None, tb, qk_w), lambda i, s: (i, s, ZC_KG // qk_w)),
            pl.BlockSpec((None, tb, v_w), lambda i, s: (i, s, ZC_VG // v_w)),
            pl.BlockSpec((None, tb, v_w), lambda i, s: (i, s, ZC_RG // v_w)),
            pl.BlockSpec((None, tb, LANES), lambda i, s: (i, s, 0)),
            pl.BlockSpec((None, LANES, qk_w), lambda i, s: (l, 0, 0)),
            pl.BlockSpec((None, 1, qk_w), lambda i, s: (l, 0, 0)),
            pl.BlockSpec((None, 1, GLA_DV), lambda i, s: (l, 0, 0)),
        ],
        out_specs=[
            pl.BlockSpec((None, tb, v_w), lambda i, s: (i, s, 0)),
            pl.BlockSpec((None, GLA_HEADS, GLA_DK, GLA_DV), lambda i, s: (i, 0, 0, 0)),
        ],
        out_shape=[jax.ShapeDtypeStruct((b, t, v_w), BF16),
                   jax.ShapeDtypeStruct((b, GLA_HEADS, GLA_DK, GLA_DV), F32)],
        scratch_shapes=[pltpu.VMEM((GLA_HEADS, GLA_DK, GLA_DV), F32)],
        compiler_params=_params(("arbitrary", "arbitrary")),
        name="gla_prompt",
    )(z3, z3, z3, z3, glr3, w2p, bg, gn)


def _gla_sample_kernel(q_ref, k_ref, v_ref, r_ref, glr_ref, w2_ref, bg_ref, gn_ref, s0_ref,
                       *rest, nb, tq, layer):
    og_ref, sn_ref = rest[-2:]
    rws = nb * tq
    sh = int(np.log2(tq))
    ri = lax.broadcasted_iota(jnp.int32, (rws, rws), 0)
    ci = lax.broadcasted_iota(jnp.int32, (rws, rws), 1)
    same = (ri >> sh) == (ci >> sh)
    causal = same & (ci <= ri)
    x = _dot_hi(glr_ref[...], w2_ref[...]) + bg_ref[...]
    g = _log_sigmoid(x) * (1.0 / GLA_TAU)
    b = _dot_exact(causal.astype(BF16), g)
    bl = _dot_exact(same.astype(BF16), g)
    q = q_ref[...] * (GLA_DK ** -0.5)
    k = k_ref[...]
    qd = (q * jnp.exp(b)).astype(BF16)
    kd = k * jnp.exp(-b)
    ke = k * jnp.exp(bl - b)
    v = v_ref[...].astype(BF16)
    a = jnp.where(causal, _dot_nt(qd, kd), 0.0)
    o = _dot(a, v)
    ke_t = ke.T
    dec_t = jnp.exp(bl).T
    row_seq = lax.broadcasted_iota(jnp.int32, (rws, GLA_DV), 0) >> sh
    col_seq = lax.broadcasted_iota(jnp.int32, (GLA_DK, rws), 1) >> sh
    for s in range(nb):
        s0 = s0_ref[s]
        o = o + jnp.where(row_seq == s, _dot(qd, s0), 0.0)
        lhs = jnp.where(col_seq == s, ke_t, 0.0).astype(BF16)
        ds = jnp.dot(lhs, v, preferred_element_type=F32)
        s_new = dec_t[:, s * tq:s * tq + 1] * s0 + ds
        if layer is None:
            sn_ref[s] = s_new
        else:
            sn_ref[layer, s] = s_new
    if layer is not None:
        for other in range(sn_ref.shape[0]):
            if other != layer:
                sn_ref[other] = jnp.zeros(sn_ref.shape[1:], sn_ref.dtype)
    og_ref[...] = (_rms(o, gn_ref[...]) * _silu(r_ref[...])).astype(og_ref.dtype)


def _gla_sample(z, glr, w2p, bg, gn, state, stacked, l, nseq, tq, nb=32):
    assert nb * tq == LANES and nseq % nb == 0
    m = z.shape[0]
    depth = state.shape[0]
    prev = () if stacked is None else (stacked,)
    return pl.pallas_call(
        functools.partial(_gla_sample_kernel, nb=nb, tq=tq, layer=None if prev else l),
        grid=(nseq // nb, GLA_HEADS),
        in_specs=[
            pl.BlockSpec((LANES, GLA_DK), lambda i, h: (i, ZC_QG // GLA_DK + h)),
            pl.BlockSpec((LANES, GLA_DK), lambda i, h: (i, ZC_KG // GLA_DK + h)),
            pl.BlockSpec((LANES, GLA_DV), lambda i, h: (i, ZC_VG // GLA_DV + h)),
            pl.BlockSpec((LANES, GLA_DV), lambda i, h: (i, ZC_RG // GLA_DV + h)),
            pl.BlockSpec((LANES, LANES), lambda i, h: (i, 0)),
            pl.BlockSpec((None, LANES, GLA_DK), lambda i, h: (l, 0, h)),
            pl.BlockSpec((None, 1, GLA_DK), lambda i, h: (l, 0, h)),
            pl.BlockSpec((None, 1, GLA_DV), lambda i, h: (l, 0, 0)),
            pl.BlockSpec((None, nb, None, GLA_DK, GLA_DV), lambda i, h: (l, i, h, 0, 0)),
        ] + [pl.BlockSpec(memory_space=pl.ANY)] * len(prev),
        out_specs=[
            pl.BlockSpec((LANES, GLA_DV), lambda i, h: (i, h)),
            (pl.BlockSpec((None, nb, None, GLA_DK, GLA_DV), lambda i, h: (l, i, h, 0, 0)) if prev else
             pl.BlockSpec((depth, nb, None, GLA_DK, GLA_DV), lambda i, h: (0, i, h, 0, 0))),
        ],
        out_shape=[jax.ShapeDtypeStruct((m, GLA_HEADS * GLA_DV), BF16),
                   jax.ShapeDtypeStruct((depth, nseq, GLA_HEADS, GLA_DK, GLA_DV), F32)],
        input_output_aliases={9: 1} if prev else {},
        compiler_params=_params(("arbitrary", "arbitrary")),
        name="gla_sample",
    )(z, z, z, z, glr, w2p, bg, gn, state, *prev)


def _prompt_attention_blocks(slope, q_ref, k_ref, v_ref, accs, ms, ls, sub, *, t, branches, per_sub):
    n = LANES
    scale = ATT_DH ** -0.5
    qi = lax.broadcasted_iota(jnp.int32, (n, 2 * n), 0)
    kc = lax.broadcasted_iota(jnp.int32, (n, 2 * n), 1)
    delta = n + qi - kc
    band = (delta >= 0) & (delta <= n)
    own = kc >= n
    ones = jnp.ones((2 * n, n), BF16)

    for bi, (w, d) in enumerate(branches):
        nblk = t // (n * d)
        assert w // d == n and t % (n * d) == 0 and nblk & (nblk - 1) == 0
        bias = delta.astype(F32) * (slope * d)
        for u in range(per_sub):
            idx = sub * per_sub + u
            r = idx >> int(np.log2(nblk))
            i = idx & (nblk - 1)
            start = r + i * (n * d)
            pstart = r + jnp.maximum(i - 1, 0) * (n * d)
            if d == 1:
                rows, prows = pl.ds(start, n), pl.ds(pstart, n)
            else:
                rows, prows = pl.ds(start, n, stride=d), pl.ds(pstart, n, stride=d)
            q = q_ref[rows, :].astype(BF16)
            k2 = jnp.concatenate([k_ref[prows, :].astype(BF16), k_ref[rows, :].astype(BF16)], axis=0)
            v2 = jnp.concatenate([v_ref[prows, :].astype(BF16), v_ref[rows, :].astype(BF16)], axis=0)
            valid = band & (own | (i > 0))
            s = jnp.where(valid, _dot_nt(q, k2) * scale - bias, NEG_INF)
            m_b = jnp.max(s, axis=-1, keepdims=True)
            p = jnp.exp(s - m_b).astype(BF16)
            ol = jnp.dot(p, jnp.concatenate([v2, ones], axis=1),
                         preferred_element_type=F32)
            accs[bi][rows, :] = ol[:, :n]
            ms[bi][rows, :] = jnp.broadcast_to(m_b, (n, n))
            ls[bi][rows, :] = ol[:, n:]


def _prompt_attention_combine(o_ref, accs, ms, ls, *, t, rows_per_step=256):
    def fin(c, carry):
        rows = pl.ds(pl.multiple_of(c * rows_per_step, rows_per_step), rows_per_step)
        m_parts = [m[rows, :] for m in ms]
        m_all = functools.reduce(jnp.maximum, m_parts)
        num = jnp.zeros((rows_per_step, LANES), F32)
        den = jnp.zeros((rows_per_step, LANES), F32)
        for acc, m_b, l in zip(accs, m_parts, ls):
            a = jnp.exp(m_b - m_all)
            num = num + a * acc[rows, :]
            den = den + a * l[rows, :]
        o_ref[rows, :] = (num / den).astype(o_ref.dtype)
        return carry

    lax.fori_loop(0, t // rows_per_step, fin, 0)


def _sample_attention(sl_ref, q_ref, kn_ref, vn_ref, khi_ref, klo_ref, vhi_ref, vlo_ref,
                      o_ref, cnt_ref, bias_ref, first_step, *, tq, wbuf, hi0, branches):
    nh = ATT_HEADS
    hsh = int(np.log2(nh))
    rq = tq * nh
    cp = LANES
    scale = ATT_DH ** -0.5
    slope = sl_ref[:, :1]
    n_hi = khi_ref.shape[0] // cp
    g = max(d for _, d in branches)
    nu = klo_ref.shape[1]
    mper = cp // nu
    ush = int(np.log2(nu))
    n_lo = klo_ref.shape[0] // mper

    def iotas(width):
        row = lax.broadcasted_iota(jnp.int32, (rq, width), 0)
        lane = lax.broadcasted_iota(jnp.int32, (rq, width), 1)
        return row >> hsh, (row & (nh - 1)) == (lane & (nh - 1)), lane >> hsh

    def multiplicity(dist, head_ok):
        cnt = jnp.zeros(dist.shape, F32)
        for (w, d) in branches:
            cnt = cnt + ((dist & (d - 1)) == 0).astype(F32) * (dist <= w).astype(F32)
        return jnp.where((dist >= 0) & head_ok, cnt, 0.0)

    @pl.when(first_step)
    def _():
        t_row, head_ok, pos_in = iotas(cp * nh)
        for c in range(n_hi):
            dist = wbuf + t_row - (hi0 + c * cp + pos_in)
            cnt_ref[c] = multiplicity(dist, head_ok)
            bias_ref[c] = slope * dist.astype(F32)
        for c in range(n_lo):
            pos = g * (c * mper + (pos_in >> ush)) + (pos_in & (nu - 1))
            dist = wbuf + t_row - pos
            cnt_ref[n_hi + c] = multiplicity(dist, head_ok)
            bias_ref[n_hi + c] = slope * dist.astype(F32)

    q2 = q_ref[...].reshape(rq, ATT_DH).astype(BF16)

    def slab(ref, c, per):
        return ref[c * per:(c + 1) * per].reshape(cp * nh, ATT_DH)

    def score(kc, cnt, bias):
        return jnp.where(cnt > 0.0, _dot_nt(q2, kc) * scale - bias, NEG_INF)

    keys = [slab(khi_ref, c, cp) for c in range(n_hi)] + [slab(klo_ref, c, mper) for c in range(n_lo)]
    vals = [slab(vhi_ref, c, cp) for c in range(n_hi)] + [slab(vlo_ref, c, mper) for c in range(n_lo)]
    scores = [score(kc, cnt_ref[c], bias_ref[c]) for c, kc in enumerate(keys)]
    t_row, head_ok, pos_in = iotas(rq)
    dist = t_row - pos_in
    cnt_new = multiplicity(dist, head_ok)
    s_new = score(kn_ref[...].reshape(rq, ATT_DH), cnt_new, slope * dist.astype(F32))
    m_all = scores[0]
    for s in scores[1:]:
        m_all = jnp.maximum(m_all, s)
    m = jnp.maximum(jnp.max(m_all, axis=-1, keepdims=True), jnp.max(s_new, axis=-1, keepdims=True))
    p_new = cnt_new * jnp.exp(s_new - m)
    acc = _dot(p_new, vn_ref[...].reshape(rq, ATT_DH))
    psum = jnp.zeros((rq, cp * nh), F32)
    for c, (s, vc) in enumerate(zip(scores, vals)):
        p = cnt_ref[c] * jnp.exp(s - m)
        psum = psum + p
        acc = acc + _dot(p, vc)
    l = jnp.sum(psum, axis=-1, keepdims=True) + jnp.sum(p_new, axis=-1, keepdims=True)
    o_ref[...] = (acc / l).reshape(tq, nh, ATT_DH)


def _attn_kernel(slopes_ref, q_ref, k_ref, v_ref, sl_ref, qs_ref, kn_ref, vn_ref,
                 khi_ref, klo_ref, vhi_ref, vlo_ref, o_ref, os_ref, *scratch,
                 t, branches, nsub, tq, wbuf, hi0):
    accs, ms, ls = scratch[0:3], scratch[3:6], scratch[6:9]
    cnt_ref, bias_ref = scratch[9:11]
    sub = pl.program_id(2)
    first_step = (pl.program_id(0) == 0) & (pl.program_id(1) == 0) & (sub == 0)
    _sample_attention(sl_ref, qs_ref, kn_ref, vn_ref, khi_ref, klo_ref, vhi_ref, vlo_ref,
                      os_ref, cnt_ref, bias_ref, first_step,
                      tq=tq, wbuf=wbuf, hi0=hi0, branches=branches)
    _prompt_attention_blocks(slopes_ref[pl.program_id(1)], q_ref, k_ref, v_ref, accs, ms, ls, sub,
                             t=t, branches=branches, per_sub=t // LANES // nsub)

    @pl.when(sub == nsub - 1)
    def _():
        _prompt_attention_combine(o_ref, accs, ms, ls, t=t)


def _attn(slopes, slope_rows, z3, q, kn, vn, cache_k, cache_v, l):
    b, t, _ = z3.shape
    nseq, tq, nh, dh = q.shape
    depth, _, wbuf, _, _ = cache_k.shape
    nsub = nseq // (b * nh)
    assert nsub * b * nh == nseq and (t // LANES) % nsub == 0
    g = max(d for _, d in DILATED_BRANCHES)
    w_dense = max(w for w, d in DILATED_BRANCHES if d < g)
    assert nh == ATT_HEADS and dh == ATT_DH and tq <= g and LANES % tq == 0
    assert all(d & (d - 1) == 0 for _, d in DILATED_BRANCHES)
    assert wbuf % w_dense == 0 and wbuf % g == 0 and w_dense % LANES == 0
    assert wbuf >= max(w for w, _ in DILATED_BRANCHES)
    hi0 = wbuf - w_dense
    n_lo = hi0 // g
    assert (n_lo * tq) % LANES == 0
    n_chunks = w_dense // LANES + n_lo * tq // LANES
    ck = cache_k.reshape(depth, nseq, wbuf // g, g, nh, dh)
    cv = cache_v.reshape(depth, nseq, wbuf // g, g, nh, dh)
    seq = lambda i, h, s: (i * nh + h) * nsub + s
    prm = lambda c0: pl.BlockSpec((None, t, dh), lambda i, h, s, sl: (i, 0, c0 // dh + h))
    new = pl.BlockSpec((None, tq, nh, dh), lambda i, h, s, sl: (seq(i, h, s), 0, 0, 0))
    hi = pl.BlockSpec((None, None, w_dense, nh, dh),
                      lambda i, h, s, sl: (l, seq(i, h, s), wbuf // w_dense - 1, 0, 0))
    lo = pl.BlockSpec((None, None, n_lo, tq, nh, dh), lambda i, h, s, sl: (l, seq(i, h, s), 0, 0, 0, 0))
    return pl.pallas_call(
        functools.partial(_attn_kernel, t=t, branches=DILATED_BRANCHES, nsub=nsub,
                          tq=tq, wbuf=wbuf, hi0=hi0),
        grid_spec=pltpu.PrefetchScalarGridSpec(
            num_scalar_prefetch=1,
            grid=(b, nh, nsub),
            in_specs=[prm(ZC_QA), prm(ZC_KA), prm(ZC_VA),
                      pl.BlockSpec((tq * nh, LANES), lambda i, h, s, sl: (0, 0)),
                      new, new, new, hi, lo, hi, lo],
            out_specs=[pl.BlockSpec((None, t, dh), lambda i, h, s, sl: (i, 0, h)), new],
            scratch_shapes=([pltpu.VMEM((t, dh), F32)] * 9
                            + [pltpu.VMEM((n_chunks, tq * nh, LANES * nh), F32)] * 2),
        ),
        out_shape=[jax.ShapeDtypeStruct((b, t, nh * dh), BF16),
                   jax.ShapeDtypeStruct((nseq, tq, nh, dh), F32)],
        compiler_params=_params(("arbitrary", "arbitrary", "arbitrary")),
        name="attn",
    )(slopes, z3, z3, z3, slope_rows, q, kn, vn, cache_k, ck, cache_v, cv)


def _outproj_kernel(x_ref, og_ref, oa_ref, wt_ref, wb_ref, o_ref):
    o_ref[...] = (x_ref[...]
                  + jnp.dot(og_ref[...], wt_ref[...], preferred_element_type=F32)
                  + jnp.dot(oa_ref[...], wb_ref[...], preferred_element_type=F32))


def _outproj(x, og, oa, w_out, l, tm, tn=512):
    m, d = x.shape
    half = og.shape[1]
    return pl.pallas_call(
        _outproj_kernel,
        grid=(m // tm, d // tn),
        in_specs=[
            pl.BlockSpec((tm, tn), lambda i, j: (i, j)),
            pl.BlockSpec((tm, half), lambda i, j: (i, 0)),
            pl.BlockSpec((tm, half), lambda i, j: (i, 0)),
            pl.BlockSpec((None, half, tn), lambda i, j: (l, 0, j)),
            pl.BlockSpec((None, half, tn), lambda i, j: (l, 1, j)),
        ],
        out_specs=pl.BlockSpec((tm, tn), lambda i, j: (i, j)),
        out_shape=jax.ShapeDtypeStruct((m, d), F32),
        compiler_params=_params(("arbitrary", "arbitrary")),
        name="outproj",
    )(x, og, oa, w_out, w_out)


def _mlp_kernel(x_ref, g_ref, wu_ref, wd_ref, *rest, na, tf, tn, emit):
    if emit:
        o_ref, wub_ref, wdb_ref, xn_ref, h_ref = rest
    else:
        o_ref, xn_ref, h_ref = rest
        wub_ref, wdb_ref = wu_ref, wd_ref
    j = pl.program_id(1)
    half = x_ref.shape[0] // 2
    parts = (slice(0, half), slice(half, None)) if emit else (slice(None),)

    @pl.when(j == 0)
    def _():
        xn_ref[...] = _rms(x_ref[...], g_ref[...]).astype(BF16)

    @pl.when(j < na)
    def _():
        if emit:
            wub_ref[...] = wu_ref[...].astype(BF16)
        u = jnp.concatenate([jnp.dot(xn_ref[rs, :], wub_ref[...], preferred_element_type=F32)
                             for rs in parts], axis=0)
        h_ref[j] = jnp.square(jnp.maximum(u, 0.0)).astype(BF16)

    @pl.when(j >= na)
    def _():
        if emit:
            wdb_ref[...] = wd_ref[...].astype(BF16)
        cols = pl.ds(pl.multiple_of((j - na) * tn, tn), tn)
        accs = [x_ref[rs, cols] for rs in parts]
        for c in range(na):
            w_c = wdb_ref[c * tf:(c + 1) * tf, :]
            accs = [acc + jnp.dot(h_ref[c, rs, :], w_c, preferred_element_type=F32)
                    for acc, rs in zip(accs, parts)]
        o_ref[...] = jnp.concatenate(accs, axis=0)


def _mlp(x, g, w_up, w_down, l, tm, tf, tn, emit):
    m, d = x.shape
    ff = w_up.shape[-1]
    na, nb = ff // tf, d // tn
    assert not emit or m == tm
    up_idx = lambda i, j: (0, jnp.minimum(j, na - 1))
    dn_idx = lambda i, j: (0, jnp.maximum(j - na, 0))
    if emit:
        w_specs = [pl.BlockSpec((None, d, tf), lambda i, j: (l,) + up_idx(i, j)),
                   pl.BlockSpec((None, ff, tn), lambda i, j: (l,) + dn_idx(i, j))]
    else:
        w_specs = [pl.BlockSpec((d, tf), up_idx), pl.BlockSpec((ff, tn), dn_idx)]
    out_specs = [pl.BlockSpec((tm, tn), lambda i, j: (i, jnp.maximum(j - na, 0)))]
    out_shape = [jax.ShapeDtypeStruct((m, d), F32)]
    if emit:
        out_specs += [pl.BlockSpec((d, tf), up_idx), pl.BlockSpec((ff, tn), dn_idx)]
        out_shape += [jax.ShapeDtypeStruct((d, ff), BF16), jax.ShapeDtypeStruct((ff, d), BF16)]
    return pl.pallas_call(
        functools.partial(_mlp_kernel, na=na, tf=tf, tn=tn, emit=emit),
        grid=(m // tm, na + nb),
        in_specs=[
            pl.BlockSpec((tm, d), lambda i, j: (i, 0), pipeline_mode=pl.Buffered(1)),
            pl.BlockSpec((None, 1, d), lambda i, j: (l, 0, 0)),
        ] + w_specs,
        out_specs=out_specs,
        out_shape=out_shape,
        scratch_shapes=[pltpu.VMEM((tm, d), BF16), pltpu.VMEM((na, tm, tf), BF16)],
        compiler_params=_params(("arbitrary", "arbitrary")),
        name="mlp",
    )(x, g, w_up, w_down)


def _ple_kernel(x_ref, g_ref, p_ref, wg_ref, wp_ref, gf_ref, o_ref, xn_ref, *, tn, final):
    j = pl.program_id(1)

    @pl.when(j == 0)
    def _():
        xn_ref[...] = _rms(x_ref[...], g_ref[...]).astype(BF16)

    cols = pl.ds(pl.multiple_of(j * tn, tn), tn)
    gate = jax.nn.sigmoid(jnp.dot(xn_ref[...], wg_ref[...], preferred_element_type=F32))
    o_ref[:, cols] = x_ref[:, cols] + gate * _dot(p_ref[...], wp_ref[...])

    if final:
        @pl.when(j == pl.num_programs(1) - 1)
        def _():
            o_ref[...] = _rms(o_ref[...], gf_ref[...])


def _ple(x, g, p, w_gate, w_proj, g_final, l, tm, final, tn=512):
    m, d = x.shape
    pd = p.shape[-1]
    return pl.pallas_call(
        functools.partial(_ple_kernel, tn=tn, final=final),
        grid=(m // tm, d // tn),
        in_specs=[
            pl.BlockSpec((tm, d), lambda i, j: (i, 0)),
            pl.BlockSpec((None, 1, d), lambda i, j: (l, 0, 0)),
            pl.BlockSpec((None, tm, pd), lambda i, j: (l, i, 0)),
            pl.BlockSpec((None, d, tn), lambda i, j: (l, 0, j)),
            pl.BlockSpec((None, pd, tn), lambda i, j: (l, 0, j)),
            pl.BlockSpec((1, d), lambda i, j: (0, 0)),
        ],
        out_specs=pl.BlockSpec((tm, d), lambda i, j: (i, 0)),
        out_shape=jax.ShapeDtypeStruct((m, d), F32),
        scratch_shapes=[pltpu.VMEM((tm, d), BF16)],
        compiler_params=_params(("arbitrary", "arbitrary")),
        name="ple",
    )(x, g, p, w_gate, w_proj, g_final)


def kernel(x_prompt, x_sample, cache_k_win, cache_v_win, state_gla, p_prompt, p_sample,
           norm_mix, w_in, w_gate2, b_gate, gla_norm, w_out, norm_mlp, w_up, w_down,
           norm_ple, w_ple_gate, w_ple_proj, norm_final):
    bp, tp, d = x_prompt.shape
    bs, ts, _ = x_sample.shape
    depth = w_in.shape[0]
    mp, ms = bp * tp, bs * ts
    tm_p, tm_s = 1024, ms
    w_keep = min(MAX_WINDOW, tp)

    slopes_np = (2.0 ** (-8.0 * np.arange(1, ATT_HEADS + 1) / ATT_HEADS)).astype(np.float32)
    slopes = jnp.asarray(slopes_np)
    slope_rows = jnp.asarray(np.broadcast_to(np.tile(slopes_np, ts)[:, None], (ts * ATT_HEADS, LANES)))
    lr0 = GLA_PART
    w_z = jnp.swapaxes(w_in, 1, 2).astype(BF16)
    w_att = w_z[:, lr0 + GATE_RANK:]
    w_lr = jnp.pad(w_z[:, lr0:lr0 + GATE_RANK], ((0, 0), (0, LANES - GATE_RANK), (0, 0)))
    w2p = jnp.pad(w_gate2, ((0, 0), (0, LANES - GATE_RANK), (0, 0)))
    w_out_b = w_out.astype(BF16)
    w_pg_b, w_pp_b = w_ple_gate.astype(BF16), w_ple_proj.astype(BF16)
    row = lambda a: a.reshape(depth, 1, a.shape[-1])
    nmix, nmlp, nple, bg, gn = row(norm_mix), row(norm_mlp), row(norm_ple), row(b_gate), row(gla_norm)
    gfin = norm_final.reshape(1, d)
    pp = p_prompt.reshape(depth, mp, -1)
    ps = p_sample.reshape(depth, ms, -1)

    hp = x_prompt.reshape(mp, d)
    hs = x_sample.reshape(ms, d)
    kp_l, vp_l, gp_l, ks_l, vs_l = [], [], [], [], []
    gs = None
    for l in range(depth):
        last = l == depth - 1
        z, glr = _inproj(hp, nmix, w_z, w_att, w_lr, l, tm_p)
        z3 = z.reshape(bp, tp, Z_COLS)
        og, s_new = _gla_prompt(z3, glr.reshape(bp, tp, LANES), w2p, bg, gn, l)
        zs, glr_s = _inproj(hs, nmix, w_z, w_att, w_lr, l, tm_s)
        og_s, gs = _gla_sample(zs, glr_s, w2p, bg, gn, state_gla, gs, l, bs, ts)
        heads = lambda c0: zs[:, c0:c0 + ATT_HEADS * ATT_DH].reshape(bs, ts, ATT_HEADS, ATT_DH)
        q_s, k_s, v_s = heads(ZC_QA), heads(ZC_KA), heads(ZC_VA)
        oa, oa_s = _attn(slopes, slope_rows, z3, q_s, k_s, v_s, cache_k_win, cache_v_win, l)
        kp_l.append(z3[:, tp - w_keep:, ZC_KA:ZC_VA].reshape(bp, w_keep, ATT_HEADS, ATT_DH))
        vp_l.append(z3[:, tp - w_keep:, ZC_VA:].reshape(bp, w_keep, ATT_HEADS, ATT_DH))
        gp_l.append(s_new)
        ks_l.append(k_s)
        vs_l.append(v_s)
        hs = _outproj(hs, og_s, oa_s.reshape(ms, -1).astype(BF16), w_out_b, l, tm_s)
        hs, w_up_b, w_down_b = _mlp(hs, nmlp, w_up, w_down, l, tm_s, tf=256, tn=256, emit=True)
        hs = _ple(hs, nple, ps, w_pg_b, w_pp_b, gfin, l, tm_s, last)
        hp = _outproj(hp, og.reshape(mp, -1), oa.reshape(mp, -1), w_out_b, l, tm_p)
        hp, = _mlp(hp, nmlp, w_up_b, w_down_b, l, tm_p, tf=512, tn=512, emit=False)
        hp = _ple(hp, nple, pp, w_pg_b, w_pp_b, gfin, l, tm_p, last)
    y_prompt = hp.reshape(bp, tp, d)
    y_sample = hs.reshape(bs, ts, d)
    return (y_prompt, y_sample, jnp.stack(kp_l), jnp.stack(vp_l), jnp.stack(gp_l),
            jnp.stack(ks_l), jnp.stack(vs_l), gs)
```

```python
import functools

import jax
import jax.numpy as jnp
import numpy as np
from jax import lax
from jax.experimental import pallas as pl
from jax.experimental.pallas import tpu as pltpu

F32 = jnp.float32
BF16 = jnp.bfloat16

EPS = 1e-6
NEG_INF = -1e30
GLA_HEADS = 4
GLA_DK = 128
GLA_DV = 256
GLA_TAU = 16.0
GLA_CHUNK = 64
ATT_HEADS = 8
ATT_DH = 128
DILATED_BRANCHES = ((128, 1), (512, 4), (2048, 16))
MAX_WINDOW = 2048
GATE_RANK = 16

LANES = 128
VMEM_LIMIT = 56 * 1024 * 1024

Z_COLS = 6144
GLA_PART = 3072
ZC_QG, ZC_KG, ZC_VG, ZC_RG, ZC_QA, ZC_KA, ZC_VA = 0, 512, 1024, 2048, 3072, 4096, 5120


def _params(sem, vmem=VMEM_LIMIT):
    return pltpu.CompilerParams(dimension_semantics=sem, vmem_limit_bytes=vmem)


def _rms(x, g):
    return x * lax.rsqrt(jnp.mean(x * x, axis=-1, keepdims=True) + EPS) * g


def _dot(a, b):
    return jnp.dot(a.astype(BF16), b.astype(BF16), preferred_element_type=F32)


def _dot_nt(a, b):
    return lax.dot_general(a.astype(BF16), b.astype(BF16), (((1,), (1,)), ((), ())),
                           preferred_element_type=F32)


def _split2(x):
    hi = x.astype(BF16)
    lo = (x - hi.astype(F32)).astype(BF16)
    return hi, lo


def _dot_hi(a, b):
    ah, al = _split2(a)
    bh, bl = _split2(b)
    d = lambda u, v: jnp.dot(u, v, preferred_element_type=F32)
    return d(ah, bh) + d(ah, bl) + d(al, bh)


def _dot_exact(m, x):
    hi = x.astype(BF16)
    r1 = x - hi.astype(F32)
    mid = r1.astype(BF16)
    lo = (r1 - mid.astype(F32)).astype(BF16)
    d = lambda v: jnp.dot(m, v, preferred_element_type=F32)
    return d(hi) + d(mid) + d(lo)


def _log_sigmoid(x):
    return jnp.minimum(x, 0.0) - jnp.log1p(jnp.exp(-jnp.abs(x)))


def _silu(x):
    return x * jax.nn.sigmoid(x)


def _inproj_kernel(x_ref, g_ref, wa_ref, wb_ref, wl_ref, z_ref, glr_ref, xn_ref, *, nja):
    j = pl.program_id(1)

    @pl.when(j == 0)
    def _():
        xb = _rms(x_ref[...], g_ref[...]).astype(BF16)
        xn_ref[...] = xb
        glr_ref[...] = _dot_nt(xb, wl_ref[...])

    @pl.when(j < nja)
    def _():
        z_ref[...] = _dot_nt(xn_ref[...], wa_ref[...])

    @pl.when(j >= nja)
    def _():
        z_ref[...] = _dot_nt(xn_ref[...], wb_ref[...])


def _inproj(x, g, w_t, w_att, w_lr, l, tm, tn=1024):
    m, d = x.shape
    assert GLA_PART % tn == 0 and Z_COLS % tn == 0
    nja = GLA_PART // tn
    return pl.pallas_call(
        functools.partial(_inproj_kernel, nja=nja),
        grid=(m // tm, Z_COLS // tn),
        in_specs=[
            pl.BlockSpec((tm, d), lambda i, j: (i, 0)),
            pl.BlockSpec((None, 1, d), lambda i, j: (l, 0, 0)),
            pl.BlockSpec((None, tn, d), lambda i, j: (l, jnp.minimum(j, nja - 1), 0)),
            pl.BlockSpec((None, tn, d), lambda i, j: (l, jnp.maximum(j - nja, 0), 0)),
            pl.BlockSpec((None, LANES, d), lambda i, j: (l, 0, 0)),
        ],
        out_specs=[
            pl.BlockSpec((tm, tn), lambda i, j: (i, j)),
            pl.BlockSpec((tm, LANES), lambda i, j: (i, 0)),
        ],
        out_shape=[jax.ShapeDtypeStruct((m, Z_COLS), F32),
                   jax.ShapeDtypeStruct((m, LANES), F32)],
        scratch_shapes=[pltpu.VMEM((tm, d), BF16)],
        compiler_params=_params(("arbitrary", "arbitrary")),
        name="inproj",
    )(x, g, w_t, w_att, w_lr)


def _gla_prompt_kernel(q_ref, k_ref, v_ref, r_ref, glr_ref, w2_ref, bg_ref, gn_ref,
                       og_ref, sout_ref, s_ref, *, tb, chunk):
    step = pl.program_id(1)

    @pl.when(step == 0)
    def _():
        s_ref[...] = jnp.zeros_like(s_ref)

    c = chunk
    ri = lax.broadcasted_iota(jnp.int32, (c, c), 0)
    ci = lax.broadcasted_iota(jnp.int32, (c, c), 1)
    causal = ci <= ri
    csh = int(np.log2(c))
    ri2 = lax.broadcasted_iota(jnp.int32, (LANES, LANES), 0)
    ci2 = lax.broadcasted_iota(jnp.int32, (LANES, LANES), 1)
    same_chunk = (ri2 >> csh) == (ci2 >> csh)
    same2 = same_chunk.astype(BF16)
    tri2 = (same_chunk & (ci2 <= ri2)).astype(BF16)
    w2 = w2_ref[...]
    bg = bg_ref[...]
    gn = gn_ref[...]
    zpad_k = jnp.zeros((LANES - c, GLA_DK), F32)
    zpad_v = jnp.zeros((LANES - c, GLA_DV), BF16)

    g_all = _log_sigmoid(_dot_hi(glr_ref[...], w2) + bg) * (1.0 / GLA_TAU)
    b_parts, bl_parts = [], []
    for i in range(tb // LANES):
        g_pair = g_all[i * LANES:(i + 1) * LANES, :]
        b_parts.append(_dot_exact(tri2, g_pair))
        bl_parts.append(_dot_exact(same2, g_pair))
    b_all = jnp.concatenate(b_parts, axis=0)
    bl_all = jnp.concatenate(bl_parts, axis=0)
    e_q = jnp.exp(b_all)
    e_k = jnp.exp(-b_all)
    e_end = jnp.exp(bl_all - b_all)
    e_dec = jnp.exp(bl_all)
    state = [s_ref[h] for h in range(GLA_HEADS)]
    for n in range(tb // c):
        rows = slice(n * c, (n + 1) * c)
        for h in range(GLA_HEADS):
            ks = slice(h * GLA_DK, (h + 1) * GLA_DK)
            vs = slice(h * GLA_DV, (h + 1) * GLA_DV)
            q = q_ref[rows, ks] * (GLA_DK ** -0.5)
            k = k_ref[rows, ks]
            v = v_ref[rows, vs].astype(BF16)
            qd = (q * e_q[rows, ks]).astype(BF16)
            kd = k * e_k[rows, ks]
            ke = k * e_end[rows, ks]
            a = jnp.where(causal, _dot_nt(qd, kd), 0.0)
            s_old = state[h]
            o = _dot(a, v) + _dot(qd, s_old)
            ke_t = jnp.concatenate([ke, zpad_k], axis=0).T.astype(BF16)
            v_pad = jnp.concatenate([v, zpad_v], axis=0)
            ds = jnp.dot(ke_t, v_pad, preferred_element_type=F32)
            dec = jnp.broadcast_to(e_dec[n * c:n * c + 1, ks], (GLA_DK, GLA_DK)).T
            state[h] = jnp.concatenate([dec, dec], axis=1) * s_old + ds
            on = _rms(o, gn)
            og_ref[rows, vs] = (on * _silu(r_ref[rows, vs])).astype(og_ref.dtype)
    for h in range(GLA_HEADS):
        s_ref[h] = state[h]

    @pl.when(step == pl.num_programs(1) - 1)
    def _():
        sout_ref[...] = s_ref[...]


def _gla_prompt(z3, glr3, w2p, bg, gn, l, tb=512):
    b, t, _ = z3.shape
    chunk = np.gcd(t, GLA_CHUNK)
    assert t % tb == 0 and tb % LANES == 0 and LANES % chunk == 0 and chunk & (chunk - 1) == 0
    qk_w = GLA_HEADS * GLA_DK
    v_w = GLA_HEADS * GLA_DV
    return pl.pallas_call(
        functools.partial(_gla_prompt_kernel, tb=tb, chunk=int(chunk)),
        grid=(b, t // tb),
        in_specs=[
            pl.BlockSpec((None, tb, qk_w), lambda i, s: (i, s, ZC_QG // qk_w)),
            pl.BlockSpec((None, tb, qk_w), lambda i, s: (i, s, ZC_KG // qk_w)),
            pl.BlockSpec((None, tb, v_w), lambda i, s: (i, s, ZC_VG // v_w)),
            pl.BlockSpec((None, tb, v_w), lambda i, s: (i, s, ZC_RG // v_w)),
            pl.BlockSpec((None, tb, LANES), lambda i, s: (i, s, 0)),
            pl.BlockSpec((None, LANES, qk_w), lambda i, s: (l, 0, 0)),
            pl.BlockSpec((None, 1, qk_w), lambda i, s: (l, 0, 0)),
            pl.BlockSpec((None, 1, GLA_DV), lambda i, s: (l, 0, 0)),
        ],
        out_specs=[
            pl.BlockSpec((None, tb, v_w), lambda i, s: (i, s, 0)),
            pl.BlockSpec((None, GLA_HEADS, GLA_DK, GLA_DV), lambda i, s: (i, 0, 0, 0)),
        ],
        out_shape=[jax.ShapeDtypeStruct((b, t, v_w), BF16),
                   jax.ShapeDtypeStruct((b, GLA_HEADS, GLA_DK, GLA_DV), F32)],
        scratch_shapes=[pltpu.VMEM((GLA_HEADS, GLA_DK, GLA_DV), F32)],
        compiler_params=_params(("arbitrary", "arbitrary")),
        name="gla_prompt",
    )(z3, z3, z3, z3, glr3, w2p, bg, gn)


def _gla_sample_kernel(q_ref, k_ref, v_ref, r_ref, glr_ref, w2_ref, bg_ref, gn_ref, s0_ref,
                       *rest, nb, tq, layer):
    og_ref, sn_ref = rest[-2:]
    rws = nb * tq
    sh = int(np.log2(tq))
    ri = lax.broadcasted_iota(jnp.int32, (rws, rws), 0)
    ci = lax.broadcasted_iota(jnp.int32, (rws, rws), 1)
    same = (ri >> sh) == (ci >> sh)
    causal = same & (ci <= ri)
    x = _dot_hi(glr_ref[...], w2_ref[...]) + bg_ref[...]
    g = _log_sigmoid(x) * (1.0 / GLA_TAU)
    b = _dot_exact(causal.astype(BF16), g)
    bl = _dot_exact(same.astype(BF16), g)
    q = q_ref[...] * (GLA_DK ** -0.5)
    k = k_ref[...]
    qd = (q * jnp.exp(b)).astype(BF16)
    kd = k * jnp.exp(-b)
    ke = k * jnp.exp(bl - b)
    v = v_ref[...].astype(BF16)
    a = jnp.where(causal, _dot_nt(qd, kd), 0.0)
    o = _dot(a, v)
    ke_t = ke.T
    dec_t = jnp.exp(bl).T
    row_seq = lax.broadcasted_iota(jnp.int32, (rws, GLA_DV), 0) >> sh
    col_seq = lax.broadcasted_iota(jnp.int32, (GLA_DK, rws), 1) >> sh
    for s in range(nb):
        s0 = s0_ref[s]
        o = o + jnp.where(row_seq == s, _dot(qd, s0), 0.0)
        lhs = jnp.where(col_seq == s, ke_t, 0.0).astype(BF16)
        ds = jnp.dot(lhs, v, preferred_element_type=F32)
        s_new = dec_t[:, s * tq:s * tq + 1] * s0 + ds
        if layer is None:
            sn_ref[s] = s_new
        else:
            sn_ref[layer, s] = s_new
    if layer is not None:
        for other in range(sn_ref.shape[0]):
            if other != layer:
                sn_ref[other] = jnp.zeros(sn_ref.shape[1:], sn_ref.dtype)
    og_ref[...] = (_rms(o, gn_ref[...]) * _silu(r_ref[...])).astype(og_ref.dtype)


def _gla_sample(z, glr, w2p, bg, gn, state, stacked, l, nseq, tq, nb=32):
    assert nb * tq == LANES and nseq % nb == 0
    m = z.shape[0]
    depth = state.shape[0]
    prev = () if stacked is None else (stacked,)
    return pl.pallas_call(
        functools.partial(_gla_sample_kernel, nb=nb, tq=tq, layer=None if prev else l),
        grid=(nseq // nb, GLA_HEADS),
        in_specs=[
            pl.BlockSpec((LANES, GLA_DK), lambda i, h: (i, ZC_QG // GLA_DK + h)),
            pl.BlockSpec((LANES, GLA_DK), lambda i, h: (i, ZC_KG // GLA_DK + h)),
            pl.BlockSpec((LANES, GLA_DV), lambda i, h: (i, ZC_VG // GLA_DV + h)),
            pl.BlockSpec((LANES, GLA_DV), lambda i, h: (i, ZC_RG // GLA_DV + h)),
            pl.BlockSpec((LANES, LANES), lambda i, h: (i, 0)),
            pl.BlockSpec((None, LANES, GLA_DK), lambda i, h: (l, 0, h)),
            pl.BlockSpec((None, 1, GLA_DK), lambda i, h: (l, 0, h)),
            pl.BlockSpec((None, 1, GLA_DV), lambda i, h: (l, 0, 0)),
            pl.BlockSpec((None, nb, None, GLA_DK, GLA_DV), lambda i, h: (l, i, h, 0, 0)),
        ] + [pl.BlockSpec(memory_space=pl.ANY)] * len(prev),
        out_specs=[
            pl.BlockSpec((LANES, GLA_DV), lambda i, h: (i, h)),
            (pl.BlockSpec((None, nb, None, GLA_DK, GLA_DV), lambda i, h: (l, i, h, 0, 0)) if prev else
             pl.BlockSpec((depth, nb, None, GLA_DK, GLA_DV), lambda i, h: (0, i, h, 0, 0))),
        ],
        out_shape=[jax.ShapeDtypeStruct((m, GLA_HEADS * GLA_DV), BF16),
                   jax.ShapeDtypeStruct((depth, nseq, GLA_HEADS, GLA_DK, GLA_DV), F32)],
        input_output_aliases={9: 1} if prev else {},
        compiler_params=_params(("arbitrary", "arbitrary")),
        name="gla_sample",
    )(z, z, z, z, glr, w2p, bg, gn, state, *prev)


def _prompt_attention_blocks(slope, q_ref, k_ref, v_ref, accs, ms, ls, sub, *, t, branches, per_sub):
    n = LANES
    scale = ATT_DH ** -0.5
    qi = lax.broadcasted_iota(jnp.int32, (n, 2 * n), 0)
    kc = lax.broadcasted_iota(jnp.int32, (n, 2 * n), 1)
    delta = n + qi - kc
    band = (delta >= 0) & (delta <= n)
    own = kc >= n
    ones = jnp.ones((2 * n, n), BF16)

    for bi, (w, d) in enumerate(branches):
        nblk = t // (n * d)
        assert w // d == n and t % (n * d) == 0 and nblk & (nblk - 1) == 0
        if nblk == 2 and per_sub % 2 == 0:
            qi2 = lax.broadcasted_iota(jnp.int32, (2 * n, 2 * n), 0)
            kc2 = lax.broadcasted_iota(jnp.int32, (2 * n, 2 * n), 1)
            delta2 = qi2 - kc2
            band2 = (delta2 >= 0) & (delta2 <= n)
            bias2 = delta2.astype(F32) * (slope * d)
            for u in range(per_sub // 2):
                rows = pl.ds(sub * (per_sub // 2) + u, 2 * n, stride=d)
                q = q_ref[rows, :].astype(BF16)
                s = jnp.where(band2, _dot_nt(q, k_ref[rows, :]) * scale - bias2, NEG_INF)
                m_b = jnp.max(s, axis=-1, keepdims=True)
                p = jnp.exp(s - m_b).astype(BF16)
                ol = jnp.dot(p, jnp.concatenate([v_ref[rows, :].astype(BF16), ones], axis=1),
                             preferred_element_type=F32)
                accs[bi][rows, :] = ol[:, :n]
                ms[bi][rows, :] = jnp.broadcast_to(m_b, (2 * n, n))
                ls[bi][rows, :] = ol[:, n:]
            continue
        bias = delta.astype(F32) * (slope * d)
        for u in range(per_sub):
            idx = sub * per_sub + u
            r = idx >> int(np.log2(nblk))
            i = idx & (nblk - 1)
            start = r + i * (n * d)
            pstart = r + jnp.maximum(i - 1, 0) * (n * d)
            if d == 1:
                rows, prows = pl.ds(start, n), pl.ds(pstart, n)
            else:
                rows, prows = pl.ds(start, n, stride=d), pl.ds(pstart, n, stride=d)
            q = q_ref[rows, :].astype(BF16)
            k2 = jnp.concatenate([k_ref[prows, :].astype(BF16), k_ref[rows, :].astype(BF16)], axis=0)
            v2 = jnp.concatenate([v_ref[prows, :].astype(BF16), v_ref[rows, :].astype(BF16)], axis=0)
            valid = band & (own | (i > 0))
            s = jnp.where(valid, _dot_nt(q, k2) * scale - bias, NEG_INF)
            m_b = jnp.max(s, axis=-1, keepdims=True)
            p = jnp.exp(s - m_b).astype(BF16)
            ol = jnp.dot(p, jnp.concatenate([v2, ones], axis=1),
                         preferred_element_type=F32)
            accs[bi][rows, :] = ol[:, :n]
            ms[bi][rows, :] = jnp.broadcast_to(m_b, (n, n))
            ls[bi][rows, :] = ol[:, n:]


def _prompt_attention_combine(o_ref, accs, ms, ls, *, t, rows_per_step=256):
    def fin(c, carry):
        rows = pl.ds(pl.multiple_of(c * rows_per_step, rows_per_step), rows_per_step)
        m_parts = [m[rows, :] for m in ms]
        m_all = functools.reduce(jnp.maximum, m_parts)
        num = jnp.zeros((rows_per_step, LANES), F32)
        den = jnp.zeros((rows_per_step, LANES), F32)
        for acc, m_b, l in zip(accs, m_parts, ls):
            a = jnp.exp(m_b - m_all)
            num = num + a * acc[rows, :]
            den = den + a * l[rows, :]
        o_ref[rows, :] = (num / den).astype(o_ref.dtype)
        return carry

    lax.fori_loop(0, t // rows_per_step, fin, 0)


def _sample_attention(sl_ref, q_ref, kn_ref, vn_ref, khi_ref, klo_ref, vhi_ref, vlo_ref,
                      o_ref, cnt_ref, bias_ref, first_step, *, tq, wbuf, hi0, branches):
    nh = ATT_HEADS
    hsh = int(np.log2(nh))
    rq = tq * nh
    cp = LANES
    scale = ATT_DH ** -0.5
    slope = sl_ref[:, :1]
    n_hi = khi_ref.shape[0] // cp
    g = max(d for _, d in branches)
    nu = klo_ref.shape[1]
    mper = cp // nu
    ush = int(np.log2(nu))
    n_lo = klo_ref.shape[0] // mper

    def iotas(width):
        row = lax.broadcasted_iota(jnp.int32, (rq, width), 0)
        lane = lax.broadcasted_iota(jnp.int32, (rq, width), 1)
        return row >> hsh, (row & (nh - 1)) == (lane & (nh - 1)), lane >> hsh

    def multiplicity(dist, head_ok):
        cnt = jnp.zeros(dist.shape, F32)
        for (w, d) in branches:
            cnt = cnt + ((dist & (d - 1)) == 0).astype(F32) * (dist <= w).astype(F32)
        return jnp.where((dist >= 0) & head_ok, cnt, 0.0)

    @pl.when(first_step)
    def _():
        t_row, head_ok, pos_in = iotas(cp * nh)
        for c in range(n_hi):
            dist = wbuf + t_row - (hi0 + c * cp + pos_in)
            cnt_ref[c] = multiplicity(dist, head_ok)
            bias_ref[c] = slope * dist.astype(F32)
        for c in range(n_lo):
            pos = g * (c * mper + (pos_in >> ush)) + (pos_in & (nu - 1))
            dist = wbuf + t_row - pos
            cnt_ref[n_hi + c] = multiplicity(dist, head_ok)
            bias_ref[n_hi + c] = slope * dist.astype(F32)

    q2 = q_ref[...].reshape(rq, ATT_DH).astype(BF16)

    def slab(ref, c, per):
        return ref[c * per:(c + 1) * per].reshape(cp * nh, ATT_DH)

    def score(kc, cnt, bias):
        return jnp.where(cnt > 0.0, _dot_nt(q2, kc) * scale - bias, NEG_INF)

    keys = [slab(khi_ref, c, cp) for c in range(n_hi)] + [slab(klo_ref, c, mper) for c in range(n_lo)]
    vals = [slab(vhi_ref, c, cp) for c in range(n_hi)] + [slab(vlo_ref, c, mper) for c in range(n_lo)]
    scores = [score(kc, cnt_ref[c], bias_ref[c]) for c, kc in enumerate(keys)]
    t_row, head_ok, pos_in = iotas(rq)
    dist = t_row - pos_in
    cnt_new = multiplicity(dist, head_ok)
    s_new = score(kn_ref[...].reshape(rq, ATT_DH), cnt_new, slope * dist.astype(F32))
    m_all = scores[0]
    for s in scores[1:]:
        m_all = jnp.maximum(m_all, s)
    m = jnp.maximum(jnp.max(m_all, axis=-1, keepdims=True), jnp.max(s_new, axis=-1, keepdims=True))
    p_new = cnt_new * jnp.exp(s_new - m)
    acc = _dot(p_new, vn_ref[...].reshape(rq, ATT_DH))
    psum = jnp.zeros((rq, cp * nh), F32)
    for c, (s, vc) in enumerate(zip(scores, vals)):
        p = cnt_ref[c] * jnp.exp(s - m)
        psum = psum + p
        acc = acc + _dot(p, vc)
    l = jnp.sum(psum, axis=-1, keepdims=True) + jnp.sum(p_new, axis=-1, keepdims=True)
    o_ref[...] = (acc / l).reshape(tq, nh, ATT_DH)


def _attn_kernel(slopes_ref, q_ref, k_ref, v_ref, sl_ref, qs_ref, kn_ref, vn_ref,
                 khi_ref, klo_ref, vhi_ref, vlo_ref, o_ref, os_ref, *scratch,
                 t, branches, nsub, tq, wbuf, hi0):
    accs, ms, ls = scratch[0:3], scratch[3:6], scratch[6:9]
    cnt_ref, bias_ref = scratch[9:11]
    sub = pl.program_id(2)
    first_step = (pl.program_id(0) == 0) & (pl.program_id(1) == 0) & (sub == 0)
    _sample_attention(sl_ref, qs_ref, kn_ref, vn_ref, khi_ref, klo_ref, vhi_ref, vlo_ref,
                      os_ref, cnt_ref, bias_ref, first_step,
                      tq=tq, wbuf=wbuf, hi0=hi0, branches=branches)
    _prompt_attention_blocks(slopes_ref[pl.program_id(1)], q_ref, k_ref, v_ref, accs, ms, ls, sub,
                             t=t, branches=branches, per_sub=t // LANES // nsub)

    @pl.when(sub == nsub - 1)
    def _():
        _prompt_attention_combine(o_ref, accs, ms, ls, t=t)


def _attn(slopes, slope_rows, z3, q, kn, vn, cache_k, cache_v, l):
    b, t, _ = z3.shape
    nseq, tq, nh, dh = q.shape
    depth, _, wbuf, _, _ = cache_k.shape
    nsub = nseq // (b * nh)
    assert nsub * b * nh == nseq and (t // LANES) % nsub == 0
    g = max(d for _, d in DILATED_BRANCHES)
    w_dense = max(w for w, d in DILATED_BRANCHES if d < g)
    assert nh == ATT_HEADS and dh == ATT_DH and tq <= g and LANES % tq == 0
    assert all(d & (d - 1) == 0 for _, d in DILATED_BRANCHES)
    assert wbuf % w_dense == 0 and wbuf % g == 0 and w_dense % LANES == 0
    assert wbuf >= max(w for w, _ in DILATED_BRANCHES)
    hi0 = wbuf - w_dense
    n_lo = hi0 // g
    assert (n_lo * tq) % LANES == 0
    n_chunks = w_dense // LANES + n_lo * tq // LANES
    ck = cache_k.reshape(depth, nseq, wbuf // g, g, nh, dh)
    cv = cache_v.reshape(depth, nseq, wbuf // g, g, nh, dh)
    seq = lambda i, h, s: (i * nh + h) * nsub + s
    prm = lambda c0: pl.BlockSpec((None, t, dh), lambda i, h, s, sl: (i, 0, c0 // dh + h))
    new = pl.BlockSpec((None, tq, nh, dh), lambda i, h, s, sl: (seq(i, h, s), 0, 0, 0))
    hi = pl.BlockSpec((None, None, w_dense, nh, dh),
                      lambda i, h, s, sl: (l, seq(i, h, s), wbuf // w_dense - 1, 0, 0))
    lo = pl.BlockSpec((None, None, n_lo, tq, nh, dh), lambda i, h, s, sl: (l, seq(i, h, s), 0, 0, 0, 0))
    return pl.pallas_call(
        functools.partial(_attn_kernel, t=t, branches=DILATED_BRANCHES, nsub=nsub,
                          tq=tq, wbuf=wbuf, hi0=hi0),
        grid_spec=pltpu.PrefetchScalarGridSpec(
            num_scalar_prefetch=1,
            grid=(b, nh, nsub),
            in_specs=[prm(ZC_QA), prm(ZC_KA), prm(ZC_VA),
                      pl.BlockSpec((tq * nh, LANES), lambda i, h, s, sl: (0, 0)),
                      new, new, new, hi, lo, hi, lo],
            out_specs=[pl.BlockSpec((None, t, dh), lambda i, h, s, sl: (i, 0, h)), new],
            scratch_shapes=([pltpu.VMEM((t, dh), F32)] * 9
                            + [pltpu.VMEM((n_chunks, tq * nh, LANES * nh), F32)] * 2),
        ),
        out_shape=[jax.ShapeDtypeStruct((b, t, nh * dh), BF16),
                   jax.ShapeDtypeStruct((nseq, tq, nh, dh), F32)],
        compiler_params=_params(("arbitrary", "arbitrary", "arbitrary")),
        name="attn",
    )(slopes, z3, z3, z3, slope_rows, q, kn, vn, cache_k, ck, cache_v, cv)


def _outproj_kernel(x_ref, og_ref, oa_ref, wt_ref, wb_ref, *rest, emit):
    if emit:
        o_ref, wtb_ref, wbb_ref = rest
        wtb_ref[...] = wt_ref[...].astype(BF16)
        wbb_ref[...] = wb_ref[...].astype(BF16)
    else:
        (o_ref,), wtb_ref, wbb_ref = rest, wt_ref, wb_ref
    o_ref[...] = (x_ref[...]
                  + jnp.dot(og_ref[...], wtb_ref[...], preferred_element_type=F32)
                  + jnp.dot(oa_ref[...], wbb_ref[...], preferred_element_type=F32))


def _outproj(x, og, oa, w_top, w_bot, l, tm, emit, tn=512):
    m, d = x.shape
    half = og.shape[1]
    assert not emit or m == tm
    if emit:
        w_specs = [pl.BlockSpec((None, half, tn), lambda i, j: (l, 0, j)),
                   pl.BlockSpec((None, half, tn), lambda i, j: (l, 1, j))]
    else:
        w_specs = [pl.BlockSpec((half, tn), lambda i, j: (0, j))] * 2
    out_specs = [pl.BlockSpec((tm, tn), lambda i, j: (i, j))]
    out_shape = [jax.ShapeDtypeStruct((m, d), F32)]
    if emit:
        out_specs += [pl.BlockSpec((half, tn), lambda i, j: (0, j))] * 2
        out_shape += [jax.ShapeDtypeStruct((half, d), BF16)] * 2
    return pl.pallas_call(
        functools.partial(_outproj_kernel, emit=emit),
        grid=(m // tm, d // tn),
        in_specs=[
            pl.BlockSpec((tm, tn), lambda i, j: (i, j)),
            pl.BlockSpec((tm, half), lambda i, j: (i, 0)),
            pl.BlockSpec((tm, half), lambda i, j: (i, 0)),
        ] + w_specs,
        out_specs=out_specs,
        out_shape=out_shape,
        compiler_params=_params(("arbitrary", "arbitrary")),
        name="outproj",
    )(x, og, oa, w_top, w_bot)


def _mlp_kernel(x_ref, g_ref, wu_ref, wd_ref, *rest, na, tf, tn, emit):
    if emit:
        o_ref, wub_ref, wdb_ref, xn_ref, h_ref = rest
    else:
        o_ref, xn_ref, h_ref = rest
        wub_ref, wdb_ref = wu_ref, wd_ref
    j = pl.program_id(1)
    half = x_ref.shape[0] // 2
    parts = (slice(0, half), slice(half, None)) if emit else (slice(None),)

    @pl.when(j == 0)
    def _():
        xn_ref[...] = _rms(x_ref[...], g_ref[...]).astype(BF16)

    @pl.when(j < na)
    def _():
        if emit:
            wub_ref[...] = wu_ref[...].astype(BF16)
        u = jnp.concatenate([jnp.dot(xn_ref[rs, :], wub_ref[...], preferred_element_type=F32)
                             for rs in parts], axis=0)
        h_ref[j] = jnp.square(jnp.maximum(u, 0.0)).astype(BF16)

    @pl.when(j >= na)
    def _():
        if emit:
            wdb_ref[...] = wd_ref[...].astype(BF16)
        cols = pl.ds(pl.multiple_of((j - na) * tn, tn), tn)
        accs = [x_ref[rs, cols] for rs in parts]
        for c in range(na):
            w_c = wdb_ref[c * tf:(c + 1) * tf, :]
            accs = [acc + jnp.dot(h_ref[c, rs, :], w_c, preferred_element_type=F32)
                    for acc, rs in zip(accs, parts)]
        o_ref[...] = jnp.concatenate(accs, axis=0)


def _mlp(x, g, w_up, w_down, l, tm, tf, tn, emit):
    m, d = x.shape
    ff = w_up.shape[-1]
    na, nb = ff // tf, d // tn
    assert not emit or m == tm
    up_idx = lambda i, j: (0, jnp.minimum(j, na - 1))
    dn_idx = lambda i, j: (0, jnp.maximum(j - na, 0))
    if emit:
        w_specs = [pl.BlockSpec((None, d, tf), lambda i, j: (l,) + up_idx(i, j)),
                   pl.BlockSpec((None, ff, tn), lambda i, j: (l,) + dn_idx(i, j))]
    else:
        w_specs = [pl.BlockSpec((d, tf), up_idx), pl.BlockSpec((ff, tn), dn_idx)]
    out_specs = [pl.BlockSpec((tm, tn), lambda i, j: (i, jnp.maximum(j - na, 0)))]
    out_shape = [jax.ShapeDtypeStruct((m, d), F32)]
    if emit:
        out_specs += [pl.BlockSpec((d, tf), up_idx), pl.BlockSpec((ff, tn), dn_idx)]
        out_shape += [jax.ShapeDtypeStruct((d, ff), BF16), jax.ShapeDtypeStruct((ff, d), BF16)]
    return pl.pallas_call(
        functools.partial(_mlp_kernel, na=na, tf=tf, tn=tn, emit=emit),
        grid=(m // tm, na + nb),
        in_specs=[
            pl.BlockSpec((tm, d), lambda i, j: (i, 0), pipeline_mode=pl.Buffered(1)),
            pl.BlockSpec((None, 1, d), lambda i, j: (l, 0, 0)),
        ] + w_specs,
        out_specs=out_specs,
        out_shape=out_shape,
        scratch_shapes=[pltpu.VMEM((tm, d), BF16), pltpu.VMEM((na, tm, tf), BF16)],
        compiler_params=_params(("arbitrary", "arbitrary")),
        name="mlp",
    )(x, g, w_up, w_down)


def _ple_kernel(x_ref, g_ref, p_ref, wg_ref, wp_ref, gf_ref, *rest, tn, final, emit):
    if emit:
        o_ref, wgb_ref, xn_ref = rest
        wgb_ref[...] = wg_ref[...].astype(BF16)
    else:
        (o_ref, xn_ref), wgb_ref = rest, wg_ref
    j = pl.program_id(1)

    @pl.when(j == 0)
    def _():
        xn_ref[...] = _rms(x_ref[...], g_ref[...]).astype(BF16)

    cols = pl.ds(pl.multiple_of(j * tn, tn), tn)
    gate = jax.nn.sigmoid(jnp.dot(xn_ref[...], wgb_ref[...], preferred_element_type=F32))
    o_ref[:, cols] = x_ref[:, cols] + gate * _dot(p_ref[...], wp_ref[...])

    if final:
        @pl.when(j == pl.num_programs(1) - 1)
        def _():
            o_ref[...] = _rms(o_ref[...], gf_ref[...])


def _ple(x, g, p, w_gate, w_proj, g_final, l, tm, final, emit, tn=512):
    m, d = x.shape
    pd = p.shape[-1]
    assert not emit or m == tm
    out_specs = [pl.BlockSpec((tm, d), lambda i, j: (i, 0))]
    out_shape = [jax.ShapeDtypeStruct((m, d), F32)]
    if emit:
        wg_spec = pl.BlockSpec((None, d, tn), lambda i, j: (l, 0, j))
        out_specs += [pl.BlockSpec((d, tn), lambda i, j: (0, j))]
        out_shape += [jax.ShapeDtypeStruct((d, d), BF16)]
    else:
        wg_spec = pl.BlockSpec((d, tn), lambda i, j: (0, j))
    return pl.pallas_call(
        functools.partial(_ple_kernel, tn=tn, final=final, emit=emit),
        grid=(m // tm, d // tn),
        in_specs=[
            pl.BlockSpec((tm, d), lambda i, j: (i, 0)),
            pl.BlockSpec((None, 1, d), lambda i, j: (l, 0, 0)),
            pl.BlockSpec((None, tm, pd), lambda i, j: (l, i, 0)),
            wg_spec,
            pl.BlockSpec((None, pd, tn), lambda i, j: (l, 0, j)),
            pl.BlockSpec((1, d), lambda i, j: (0, 0)),
        ],
        out_specs=out_specs,
        out_shape=out_shape,
        scratch_shapes=[pltpu.VMEM((tm, d), BF16)],
        compiler_params=_params(("arbitrary", "arbitrary")),
        name="ple",
    )(x, g, p, w_gate, w_proj, g_final)


def kernel(x_prompt, x_sample, cache_k_win, cache_v_win, state_gla, p_prompt, p_sample,
           norm_mix, w_in, w_gate2, b_gate, gla_norm, w_out, norm_mlp, w_up, w_down,
           norm_ple, w_ple_gate, w_ple_proj, norm_final):
    bp, tp, d = x_prompt.shape
    bs, ts, _ = x_sample.shape
    depth = w_in.shape[0]
    mp, ms = bp * tp, bs * ts
    tm_p, tm_s = 1024, ms
    w_keep = min(MAX_WINDOW, tp)

    slopes_np = (2.0 ** (-8.0 * np.arange(1, ATT_HEADS + 1) / ATT_HEADS)).astype(np.float32)
    slopes = jnp.asarray(slopes_np)
    slope_rows = jnp.asarray(np.broadcast_to(np.tile(slopes_np, ts)[:, None], (ts * ATT_HEADS, LANES)))
    lr0 = GLA_PART
    w_z = jnp.swapaxes(w_in, 1, 2).astype(BF16)
    w_att = w_z[:, lr0 + GATE_RANK:]
    w_lr = jnp.pad(w_z[:, lr0:lr0 + GATE_RANK], ((0, 0), (0, LANES - GATE_RANK), (0, 0)))
    w2p = jnp.pad(w_gate2, ((0, 0), (0, LANES - GATE_RANK), (0, 0)))
    w_pp_b = w_ple_proj.astype(BF16)
    row = lambda a: a.reshape(depth, 1, a.shape[-1])
    nmix, nmlp, nple, bg, gn = row(norm_mix), row(norm_mlp), row(norm_ple), row(b_gate), row(gla_norm)
    gfin = norm_final.reshape(1, d)
    pp = p_prompt.reshape(depth, mp, -1)
    ps = p_sample.reshape(depth, ms, -1)

    hp = x_prompt.reshape(mp, d)
    hs = x_sample.reshape(ms, d)
    kp_l, vp_l, gp_l, ks_l, vs_l = [], [], [], [], []
    gs = None
    for l in range(depth):
        last = l == depth - 1
        z, glr = _inproj(hp, nmix, w_z, w_att, w_lr, l, tm_p)
        z3 = z.reshape(bp, tp, Z_COLS)
        og, s_new = _gla_prompt(z3, glr.reshape(bp, tp, LANES), w2p, bg, gn, l)
        zs, glr_s = _inproj(hs, nmix, w_z, w_att, w_lr, l, tm_s)
        og_s, gs = _gla_sample(zs, glr_s, w2p, bg, gn, state_gla, gs, l, bs, ts)
        heads = lambda c0: zs[:, c0:c0 + ATT_HEADS * ATT_DH].reshape(bs, ts, ATT_HEADS, ATT_DH)
        q_s, k_s, v_s = heads(ZC_QA), heads(ZC_KA), heads(ZC_VA)
        oa, oa_s = _attn(slopes, slope_rows, z3, q_s, k_s, v_s, cache_k_win, cache_v_win, l)
        kp_l.append(z3[:, tp - w_keep:, ZC_KA:ZC_VA].reshape(bp, w_keep, ATT_HEADS, ATT_DH))
        vp_l.append(z3[:, tp - w_keep:, ZC_VA:].reshape(bp, w_keep, ATT_HEADS, ATT_DH))
        gp_l.append(s_new)
        ks_l.append(k_s)
        vs_l.append(v_s)
        hs, wo_top, wo_bot = _outproj(hs, og_s, oa_s.reshape(ms, -1).astype(BF16), w_out, w_out,
                                      l, tm_s, emit=True)
        hs, w_up_b, w_down_b = _mlp(hs, nmlp, w_up, w_down, l, tm_s, tf=256, tn=256, emit=True)
        hs, w_pg_b = _ple(hs, nple, ps, w_ple_gate, w_pp_b, gfin, l, tm_s, last, emit=True)
        hp, = _outproj(hp, og.reshape(mp, -1), oa.reshape(mp, -1), wo_top, wo_bot, l, tm_p, emit=False)
        hp, = _mlp(hp, nmlp, w_up_b, w_down_b, l, tm_p, tf=512, tn=512, emit=False)
        hp, = _ple(hp, nple, pp, w_pg_b, w_pp_b, gfin, l, tm_p, last, emit=False)
    y_prompt = hp.reshape(bp, tp, d)
    y_sample = hs.reshape(bs, ts, d)
    return (y_prompt, y_sample, jnp.stack(kp_l), jnp.stack(vp_l), jnp.stack(gp_l),
            jnp.stack(ks_l), jnp.stack(vs_l), gs)
```

```python
import functools

import jax
import jax.numpy as jnp
import numpy as np
from jax import lax
from jax.experimental import pallas as pl
from jax.experimental.pallas import tpu as pltpu

F32 = jnp.float32
BF16 = jnp.bfloat16

EPS = 1e-6
NEG_INF = -1e30
GLA_HEADS = 4
GLA_DK = 128
GLA_DV = 256
GLA_TAU = 16.0
GLA_CHUNK = 64
ATT_HEADS = 8
ATT_DH = 128
DILATED_BRANCHES = ((128, 1), (512, 4), (2048, 16))
MAX_WINDOW = 2048
GATE_RANK = 16

LANES = 128
VMEM_LIMIT = 56 * 1024 * 1024

Z_COLS = 6144
GLA_PART = 3072
ZC_QG, ZC_KG, ZC_VG, ZC_RG, ZC_QA, ZC_KA, ZC_VA = 0, 512, 1024, 2048, 3072, 4096, 5120


def _params(sem, vmem=VMEM_LIMIT):
    return pltpu.CompilerParams(dimension_semantics=sem, vmem_limit_bytes=vmem)


def _rms(x, g):
    return x * lax.rsqrt(jnp.mean(x * x, axis=-1, keepdims=True) + EPS) * g


def _dot(a, b):
    return jnp.dot(a.astype(BF16), b.astype(BF16), preferred_element_type=F32)


def _dot_nt(a, b):
    return lax.dot_general(a.astype(BF16), b.astype(BF16), (((1,), (1,)), ((), ())),
                           preferred_element_type=F32)


def _split2(x):
    hi = x.astype(BF16)
    lo = (x - hi.astype(F32)).astype(BF16)
    return hi, lo


def _dot_hi(a, b):
    ah, al = _split2(a)
    bh, bl = _split2(b)
    d = lambda u, v: jnp.dot(u, v, preferred_element_type=F32)
    return d(ah, bh) + d(ah, bl) + d(al, bh)


def _dot_exact(m, x):
    hi = x.astype(BF16)
    r1 = x - hi.astype(F32)
    mid = r1.astype(BF16)
    lo = (r1 - mid.astype(F32)).astype(BF16)
    d = lambda v: jnp.dot(m, v, preferred_element_type=F32)
    return d(hi) + d(mid) + d(lo)


def _log_sigmoid(x):
    return jnp.minimum(x, 0.0) - jnp.log1p(jnp.exp(-jnp.abs(x)))


def _silu(x):
    return x * jax.nn.sigmoid(x)


def _inproj_kernel(x_ref, g_ref, wa_ref, wb_ref, wl_ref, z_ref, glr_ref, xn_ref, *, nja):
    j = pl.program_id(1)

    @pl.when(j == 0)
    def _():
        xb = _rms(x_ref[...], g_ref[...]).astype(BF16)
        xn_ref[...] = xb
        glr_ref[...] = _dot_nt(xb, wl_ref[...])

    @pl.when(j < nja)
    def _():
        z_ref[...] = _dot_nt(xn_ref[...], wa_ref[...])

    @pl.when(j >= nja)
    def _():
        z_ref[...] = _dot_nt(xn_ref[...], wb_ref[...])


def _inproj(x, g, w_t, w_att, w_lr, l, tm, tn=1024):
    m, d = x.shape
    assert GLA_PART % tn == 0 and Z_COLS % tn == 0
    nja = GLA_PART // tn
    return pl.pallas_call(
        functools.partial(_inproj_kernel, nja=nja),
        grid=(m // tm, Z_COLS // tn),
        in_specs=[
            pl.BlockSpec((tm, d), lambda i, j: (i, 0)),
            pl.BlockSpec((None, 1, d), lambda i, j: (l, 0, 0)),
            pl.BlockSpec((None, tn, d), lambda i, j: (l, jnp.minimum(j, nja - 1), 0)),
            pl.BlockSpec((None, tn, d), lambda i, j: (l, jnp.maximum(j - nja, 0), 0)),
            pl.BlockSpec((None, LANES, d), lambda i, j: (l, 0, 0)),
        ],
        out_specs=[
            pl.BlockSpec((tm, tn), lambda i, j: (i, j)),
            pl.BlockSpec((tm, LANES), lambda i, j: (i, 0)),
        ],
        out_shape=[jax.ShapeDtypeStruct((m, Z_COLS), F32),
                   jax.ShapeDtypeStruct((m, LANES), F32)],
        scratch_shapes=[pltpu.VMEM((tm, d), BF16)],
        compiler_params=_params(("arbitrary", "arbitrary")),
        name="inproj",
    )(x, g, w_t, w_att, w_lr)


def _gla_prompt_kernel(q_ref, k_ref, v_ref, r_ref, glr_ref, w2_ref, bg_ref, gn_ref,
                       og_ref, sout_ref, s_ref, *, tb, chunk):
    step = pl.program_id(1)

    @pl.when(step == 0)
    def _():
        s_ref[...] = jnp.zeros_like(s_ref)

    c = chunk
    ri = lax.broadcasted_iota(jnp.int32, (c, c), 0)
    ci = lax.broadcasted_iota(jnp.int32, (c, c), 1)
    causal = ci <= ri
    csh = int(np.log2(c))
    ri2 = lax.broadcasted_iota(jnp.int32, (LANES, LANES), 0)
    ci2 = lax.broadcasted_iota(jnp.int32, (LANES, LANES), 1)
    same_chunk = (ri2 >> csh) == (ci2 >> csh)
    same2 = same_chunk.astype(BF16)
    tri2 = (same_chunk & (ci2 <= ri2)).astype(BF16)
    w2 = w2_ref[...]
    bg = bg_ref[...]
    gn = gn_ref[...]
    zpad_k = jnp.zeros((LANES - c, GLA_DK), F32)
    zpad_v = jnp.zeros((LANES - c, GLA_DV), BF16)

    g_all = _log_sigmoid(_dot_hi(glr_ref[...], w2) + bg) * (1.0 / GLA_TAU)
    b_parts, bl_parts = [], []
    for i in range(tb // LANES):
        g_pair = g_all[i * LANES:(i + 1) * LANES, :]
        b_parts.append(_dot_exact(tri2, g_pair))
        bl_parts.append(_dot_exact(same2, g_pair))
    b_all = jnp.concatenate(b_parts, axis=0)
    bl_all = jnp.concatenate(bl_parts, axis=0)
    e_q = jnp.exp(b_all)
    e_k = jnp.exp(-b_all)
    e_end = jnp.exp(bl_all - b_all)
    e_dec = jnp.exp(bl_all)
    state = [s_ref[h] for h in range(GLA_HEADS)]
    for n in range(tb // c):
        rows = slice(n * c, (n + 1) * c)
        for h in range(GLA_HEADS):
            ks = slice(h * GLA_DK, (h + 1) * GLA_DK)
            vs = slice(h * GLA_DV, (h + 1) * GLA_DV)
            q = q_ref[rows, ks] * (GLA_DK ** -0.5)
            k = k_ref[rows, ks]
            v = v_ref[rows, vs].astype(BF16)
            qd = (q * e_q[rows, ks]).astype(BF16)
            kd = k * e_k[rows, ks]
            ke = k * e_end[rows, ks]
            a = jnp.where(causal, _dot_nt(qd, kd), 0.0)
            s_old = state[h]
            o = _dot(a, v) + _dot(qd, s_old)
            ke_t = jnp.concatenate([ke, zpad_k], axis=0).T.astype(BF16)
            v_pad = jnp.concatenate([v, zpad_v], axis=0)
            ds = jnp.dot(ke_t, v_pad, preferred_element_type=F32)
            dec = jnp.broadcast_to(e_dec[n * c:n * c + 1, ks], (GLA_DK, GLA_DK)).T
            state[h] = jnp.concatenate([dec, dec], axis=1) * s_old + ds
            on = _rms(o, gn)
            og_ref[rows, vs] = (on * _silu(r_ref[rows, vs])).astype(og_ref.dtype)
    for h in range(GLA_HEADS):
        s_ref[h] = state[h]

    @pl.when(step == pl.num_programs(1) - 1)
    def _():
        sout_ref[...] = s_ref[...]


def _gla_prompt(z3, glr3, w2p, bg, gn, l, tb=512):
    b, t, _ = z3.shape
    chunk = np.gcd(t, GLA_CHUNK)
    assert t % tb == 0 and tb % LANES == 0 and LANES % chunk == 0 and chunk & (chunk - 1) == 0
    qk_w = GLA_HEADS * GLA_DK
    v_w = GLA_HEADS * GLA_DV
    return pl.pallas_call(
        functools.partial(_gla_prompt_kernel, tb=tb, chunk=int(chunk)),
        grid=(b, t // tb),
        in_specs=[
            pl.BlockSpec((None, tb, qk_w), lambda i, s: (i, s, ZC_QG // qk_w)),
            pl.BlockSpec((None, tb, qk_w), lambda i, s: (i, s, ZC_KG // qk_w)),
            pl.BlockSpec((None, tb, v_w), lambda i, s: (i, s, ZC_VG // v_w)),
            pl.BlockSpec((None, tb, v_w), lambda i, s: (i, s, ZC_RG // v_w)),
            pl.BlockSpec((None, tb, LANES), lambda i, s: (i, s, 0)),
            pl.BlockSpec((None, LANES, qk_w), lambda i, s: (l, 0, 0)),
            pl.BlockSpec((None, 1, qk_w), lambda i, s: (l, 0, 0)),
            pl.BlockSpec((None, 1, GLA_DV), lambda i, s: (l, 0, 0)),
        ],
        out_specs=[
            pl.BlockSpec((None, tb, v_w), lambda i, s: (i, s, 0)),
            pl.BlockSpec((None, GLA_HEADS, GLA_DK, GLA_DV), lambda i, s: (i, 0, 0, 0)),
        ],
        out_shape=[jax.ShapeDtypeStruct((b, t, v_w), BF16),
                   jax.ShapeDtypeStruct((b, GLA_HEADS, GLA_DK, GLA_DV), F32)],
        scratch_shapes=[pltpu.VMEM((GLA_HEADS, GLA_DK, GLA_DV), F32)],
        compiler_params=_params(("arbitrary", "arbitrary")),
        name="gla_prompt",
    )(z3, z3, z3, z3, glr3, w2p, bg, gn)


def _gla_sample_kernel(q_ref, k_ref, v_ref, r_ref, glr_ref, w2_ref, bg_ref, gn_ref, s0_ref,
                       *rest, nb, tq, layer):
    og_ref, sn_ref = rest[-2:]
    rws = nb * tq
    sh = int(np.log2(tq))
    ri = lax.broadcasted_iota(jnp.int32, (rws, rws), 0)
    ci = lax.broadcasted_iota(jnp.int32, (rws, rws), 1)
    same = (ri >> sh) == (ci >> sh)
    causal = same & (ci <= ri)
    x = _dot_hi(glr_ref[...], w2_ref[...]) + bg_ref[...]
    g = _log_sigmoid(x) * (1.0 / GLA_TAU)
    b = _dot_exact(causal.astype(BF16), g)
    bl = _dot_exact(same.astype(BF16), g)
    q = q_ref[...] * (GLA_DK ** -0.5)
    k = k_ref[...]
    qd = (q * jnp.exp(b)).astype(BF16)
    kd = k * jnp.exp(-b)
    ke = k * jnp.exp(bl - b)
    v = v_ref[...].astype(BF16)
    a = jnp.where(causal, _dot_nt(qd, kd), 0.0)
    o = _dot(a, v)
    ke_t = ke.T
    dec_t = jnp.exp(bl).T
    row_seq = lax.broadcasted_iota(jnp.int32, (rws, GLA_DV), 0) >> sh
    col_seq = lax.broadcasted_iota(jnp.int32, (GLA_DK, rws), 1) >> sh
    for s in range(nb):
        s0 = s0_ref[s]
        o = o + jnp.where(row_seq == s, _dot(qd, s0), 0.0)
        lhs = jnp.where(col_seq == s, ke_t, 0.0).astype(BF16)
        ds = jnp.dot(lhs, v, preferred_element_type=F32)
        s_new = dec_t[:, s * tq:s * tq + 1] * s0 + ds
        if layer is None:
            sn_ref[s] = s_new
        else:
            sn_ref[layer, s] = s_new
    if layer is not None:
        for other in range(sn_ref.shape[0]):
            if other != layer:
                sn_ref[other] = jnp.zeros(sn_ref.shape[1:], sn_ref.dtype)
    og_ref[...] = (_rms(o, gn_ref[...]) * _silu(r_ref[...])).astype(og_ref.dtype)


def _gla_sample(z, glr, w2p, bg, gn, state, stacked, l, nseq, tq, nb=32):
    assert nb * tq == LANES and nseq % nb == 0
    m = z.shape[0]
    depth = state.shape[0]
    prev = () if stacked is None else (stacked,)
    return pl.pallas_call(
        functools.partial(_gla_sample_kernel, nb=nb, tq=tq, layer=None if prev else l),
        grid=(nseq // nb, GLA_HEADS),
        in_specs=[
            pl.BlockSpec((LANES, GLA_DK), lambda i, h: (i, ZC_QG // GLA_DK + h)),
            pl.BlockSpec((LANES, GLA_DK), lambda i, h: (i, ZC_KG // GLA_DK + h)),
            pl.BlockSpec((LANES, GLA_DV), lambda i, h: (i, ZC_VG // GLA_DV + h)),
            pl.BlockSpec((LANES, GLA_DV), lambda i, h: (i, ZC_RG // GLA_DV + h)),
            pl.BlockSpec((LANES, LANES), lambda i, h: (i, 0)),
            pl.BlockSpec((None, LANES, GLA_DK), lambda i, h: (l, 0, h)),
            pl.BlockSpec((None, 1, GLA_DK), lambda i, h: (l, 0, h)),
            pl.BlockSpec((None, 1, GLA_DV), lambda i, h: (l, 0, 0)),
            pl.BlockSpec((None, nb, None, GLA_DK, GLA_DV), lambda i, h: (l, i, h, 0, 0)),
        ] + [pl.BlockSpec(memory_space=pl.ANY)] * len(prev),
        out_specs=[
            pl.BlockSpec((LANES, GLA_DV), lambda i, h: (i, h)),
            (pl.BlockSpec((None, nb, None, GLA_DK, GLA_DV), lambda i, h: (l, i, h, 0, 0)) if prev else
             pl.BlockSpec((depth, nb, None, GLA_DK, GLA_DV), lambda i, h: (0, i, h, 0, 0))),
        ],
        out_shape=[jax.ShapeDtypeStruct((m, GLA_HEADS * GLA_DV), BF16),
                   jax.ShapeDtypeStruct((depth, nseq, GLA_HEADS, GLA_DK, GLA_DV), F32)],
        input_output_aliases={9: 1} if prev else {},
        compiler_params=_params(("arbitrary", "arbitrary")),
        name="gla_sample",
    )(z, z, z, z, glr, w2p, bg, gn, state, *prev)


def _prompt_attention_blocks(slope, q_ref, k_ref, v_ref, accs, ms, ls, sub, *, t, branches, per_sub):
    n = LANES
    scale = ATT_DH ** -0.5
    qi = lax.broadcasted_iota(jnp.int32, (n, 2 * n), 0)
    kc = lax.broadcasted_iota(jnp.int32, (n, 2 * n), 1)
    delta = n + qi - kc
    band = (delta >= 0) & (delta <= n)
    own = kc >= n
    ones = jnp.ones((2 * n, n), BF16)

    for bi, (w, d) in enumerate(branches):
        nblk = t // (n * d)
        assert w // d == n and t % (n * d) == 0 and nblk & (nblk - 1) == 0
        if nblk == 2 and per_sub % 2 == 0:
            qi2 = lax.broadcasted_iota(jnp.int32, (2 * n, 2 * n), 0)
            kc2 = lax.broadcasted_iota(jnp.int32, (2 * n, 2 * n), 1)
            delta2 = qi2 - kc2
            band2 = (delta2 >= 0) & (delta2 <= n)
            bias2 = delta2.astype(F32) * (slope * d)
            for u in range(per_sub // 2):
                rows = pl.ds(sub * (per_sub // 2) + u, 2 * n, stride=d)
                q = q_ref[rows, :].astype(BF16)
                s = jnp.where(band2, _dot_nt(q, k_ref[rows, :]) * scale - bias2, NEG_INF)
                m_b = jnp.max(s, axis=-1, keepdims=True)
                p = jnp.exp(s - m_b).astype(BF16)
                ol = jnp.dot(p, jnp.concatenate([v_ref[rows, :].astype(BF16), ones], axis=1),
                             preferred_element_type=F32)
                accs[bi][rows, :] = ol[:, :n]
                ms[bi][rows, :] = jnp.broadcast_to(m_b, (2 * n, n))
                ls[bi][rows, :] = ol[:, n:]
            continue
        bias = delta.astype(F32) * (slope * d)
        for u in range(per_sub):
            idx = sub * per_sub + u
            r = idx >> int(np.log2(nblk))
            i = idx & (nblk - 1)
            start = r + i * (n * d)
            pstart = r + jnp.maximum(i - 1, 0) * (n * d)
            if d == 1:
                rows, prows = pl.ds(start, n), pl.ds(pstart, n)
            else:
                rows, prows = pl.ds(start, n, stride=d), pl.ds(pstart, n, stride=d)
            q = q_ref[rows, :].astype(BF16)
            k2 = jnp.concatenate([k_ref[prows, :].astype(BF16), k_ref[rows, :].astype(BF16)], axis=0)
            v2 = jnp.concatenate([v_ref[prows, :].astype(BF16), v_ref[rows, :].astype(BF16)], axis=0)
            valid = band & (own | (i > 0))
            s = jnp.where(valid, _dot_nt(q, k2) * scale - bias, NEG_INF)
            m_b = jnp.max(s, axis=-1, keepdims=True)
            p = jnp.exp(s - m_b).astype(BF16)
            ol = jnp.dot(p, jnp.concatenate([v2, ones], axis=1),
                         preferred_element_type=F32)
            accs[bi][rows, :] = ol[:, :n]
            ms[bi][rows, :] = jnp.broadcast_to(m_b, (n, n))
            ls[bi][rows, :] = ol[:, n:]


def _prompt_attention_combine(o_ref, accs, ms, ls, *, t, rows_per_step=256):
    def fin(c, carry):
        rows = pl.ds(pl.multiple_of(c * rows_per_step, rows_per_step), rows_per_step)
        m_parts = [m[rows, :] for m in ms]
        m_all = functools.reduce(jnp.maximum, m_parts)
        num = jnp.zeros((rows_per_step, LANES), F32)
        den = jnp.zeros((rows_per_step, LANES), F32)
        for acc, m_b, l in zip(accs, m_parts, ls):
            a = jnp.exp(m_b - m_all)
            num = num + a * acc[rows, :]
            den = den + a * l[rows, :]
        o_ref[rows, :] = (num / den).astype(o_ref.dtype)
        return carry

    lax.fori_loop(0, t // rows_per_step, fin, 0)


def _sample_attention(sl_ref, q_ref, kn_ref, vn_ref, khi_ref, klo_ref, vhi_ref, vlo_ref,
                      o_ref, cnt_ref, bias_ref, first_step, *, tq, wbuf, hi0, branches):
    nh = ATT_HEADS
    hsh = int(np.log2(nh))
    rq = tq * nh
    cp = LANES
    scale = ATT_DH ** -0.5
    slope = sl_ref[:, :1]
    n_hi = khi_ref.shape[0] // cp
    g = max(d for _, d in branches)
    nu = klo_ref.shape[1]
    mper = cp // nu
    ush = int(np.log2(nu))
    n_lo = klo_ref.shape[0] // mper

    def iotas(width):
        row = lax.broadcasted_iota(jnp.int32, (rq, width), 0)
        lane = lax.broadcasted_iota(jnp.int32, (rq, width), 1)
        return row >> hsh, (row & (nh - 1)) == (lane & (nh - 1)), lane >> hsh

    def multiplicity(dist, head_ok):
        cnt = jnp.zeros(dist.shape, F32)
        for (w, d) in branches:
            cnt = cnt + ((dist & (d - 1)) == 0).astype(F32) * (dist <= w).astype(F32)
        return jnp.where((dist >= 0) & head_ok, cnt, 0.0)

    @pl.when(first_step)
    def _():
        t_row, head_ok, pos_in = iotas(cp * nh)
        for c in range(n_hi):
            dist = wbuf + t_row - (hi0 + c * cp + pos_in)
            cnt_ref[c] = multiplicity(dist, head_ok)
            bias_ref[c] = slope * dist.astype(F32)
        for c in range(n_lo):
            pos = g * (c * mper + (pos_in >> ush)) + (pos_in & (nu - 1))
            dist = wbuf + t_row - pos
            cnt_ref[n_hi + c] = multiplicity(dist, head_ok)
            bias_ref[n_hi + c] = slope * dist.astype(F32)

    q2 = q_ref[...].reshape(rq, ATT_DH).astype(BF16)

    def slab(ref, c, per):
        return ref[c * per:(c + 1) * per].reshape(cp * nh, ATT_DH)

    def score(kc, cnt, bias):
        return jnp.where(cnt > 0.0, _dot_nt(q2, kc) * scale - bias, NEG_INF)

    keys = [slab(khi_ref, c, cp) for c in range(n_hi)] + [slab(klo_ref, c, mper) for c in range(n_lo)]
    vals = [slab(vhi_ref, c, cp) for c in range(n_hi)] + [slab(vlo_ref, c, mper) for c in range(n_lo)]
    scores = [score(kc, cnt_ref[c], bias_ref[c]) for c, kc in enumerate(keys)]
    t_row, head_ok, pos_in = iotas(rq)
    dist = t_row - pos_in
    cnt_new = multiplicity(dist, head_ok)
    s_new = score(kn_ref[...].reshape(rq, ATT_DH), cnt_new, slope * dist.astype(F32))
    m_all = scores[0]
    for s in scores[1:]:
        m_all = jnp.maximum(m_all, s)
    m = jnp.maximum(jnp.max(m_all, axis=-1, keepdims=True), jnp.max(s_new, axis=-1, keepdims=True))
    p_new = cnt_new * jnp.exp(s_new - m)
    acc = _dot(p_new, vn_ref[...].reshape(rq, ATT_DH))
    psum = jnp.zeros((rq, cp * nh), F32)
    for c, (s, vc) in enumerate(zip(scores, vals)):
        p = cnt_ref[c] * jnp.exp(s - m)
        psum = psum + p
        acc = acc + _dot(p, vc)
    l = jnp.sum(psum, axis=-1, keepdims=True) + jnp.sum(p_new, axis=-1, keepdims=True)
    o_ref[...] = (acc / l).reshape(tq, nh, ATT_DH)


def _attn_kernel(slopes_ref, q_ref, k_ref, v_ref, sl_ref, qs_ref, kn_ref, vn_ref,
                 khi_ref, klo_ref, vhi_ref, vlo_ref, o_ref, os_ref, *scratch,
                 t, branches, nsub, tq, wbuf, hi0):
    accs, ms, ls = scratch[0:3], scratch[3:6], scratch[6:9]
    cnt_ref, bias_ref = scratch[9:11]
    sub = pl.program_id(2)
    first_step = (pl.program_id(0) == 0) & (pl.program_id(1) == 0) & (sub == 0)
    _sample_attention(sl_ref, qs_ref, kn_ref, vn_ref, khi_ref, klo_ref, vhi_ref, vlo_ref,
                      os_ref, cnt_ref, bias_ref, first_step,
                      tq=tq, wbuf=wbuf, hi0=hi0, branches=branches)
    _prompt_attention_blocks(slopes_ref[pl.program_id(1)], q_ref, k_ref, v_ref, accs, ms, ls, sub,
                             t=t, branches=branches, per_sub=t // LANES // nsub)

    @pl.when(sub == nsub - 1)
    def _():
        _prompt_attention_combine(o_ref, accs, ms, ls, t=t)


def _attn(slopes, slope_rows, z3, q, kn, vn, cache_k, cache_v, l):
    b, t, _ = z3.shape
    nseq, tq, nh, dh = q.shape
    depth, _, wbuf, _, _ = cache_k.shape
    nsub = nseq // (b * nh)
    assert nsub * b * nh == nseq and (t // LANES) % nsub == 0
    g = max(d for _, d in DILATED_BRANCHES)
    w_dense = max(w for w, d in DILATED_BRANCHES if d < g)
    assert nh == ATT_HEADS and dh == ATT_DH and tq <= g and LANES % tq == 0
    assert all(d & (d - 1) == 0 for _, d in DILATED_BRANCHES)
    assert wbuf % w_dense == 0 and wbuf % g == 0 and w_dense % LANES == 0
    assert wbuf >= max(w for w, _ in DILATED_BRANCHES)
    hi0 = wbuf - w_dense
    n_lo = hi0 // g
    assert (n_lo * tq) % LANES == 0
    n_chunks = w_dense // LANES + n_lo * tq // LANES
    ck = cache_k.reshape(depth, nseq, wbuf // g, g, nh, dh)
    cv = cache_v.reshape(depth, nseq, wbuf // g, g, nh, dh)
    seq = lambda i, h, s: (i * nh + h) * nsub + s
    prm = lambda c0: pl.BlockSpec((None, t, dh), lambda i, h, s, sl: (i, 0, c0 // dh + h))
    new = pl.BlockSpec((None, tq, nh, dh), lambda i, h, s, sl: (seq(i, h, s), 0, 0, 0))
    hi = pl.BlockSpec((None, None, w_dense, nh, dh),
                      lambda i, h, s, sl: (l, seq(i, h, s), wbuf // w_dense - 1, 0, 0))
    lo = pl.BlockSpec((None, None, n_lo, tq, nh, dh), lambda i, h, s, sl: (l, seq(i, h, s), 0, 0, 0, 0))
    return pl.pallas_call(
        functools.partial(_attn_kernel, t=t, branches=DILATED_BRANCHES, nsub=nsub,
                          tq=tq, wbuf=wbuf, hi0=hi0),
        grid_spec=pltpu.PrefetchScalarGridSpec(
            num_scalar_prefetch=1,
            grid=(b, nh, nsub),
            in_specs=[prm(ZC_QA), prm(ZC_KA), prm(ZC_VA),
                      pl.BlockSpec((tq * nh, LANES), lambda i, h, s, sl: (0, 0)),
                      new, new, new, hi, lo, hi, lo],
            out_specs=[pl.BlockSpec((None, t, dh), lambda i, h, s, sl: (i, 0, h)), new],
            scratch_shapes=([pltpu.VMEM((t, dh), F32)] * 9
                            + [pltpu.VMEM((n_chunks, tq * nh, LANES * nh), F32)] * 2),
        ),
        out_shape=[jax.ShapeDtypeStruct((b, t, nh * dh), BF16),
                   jax.ShapeDtypeStruct((nseq, tq, nh, dh), F32)],
        compiler_params=_params(("arbitrary", "arbitrary", "arbitrary")),
        name="attn",
    )(slopes, z3, z3, z3, slope_rows, q, kn, vn, cache_k, ck, cache_v, cv)


def _outproj_kernel(x_ref, og_ref, oa_ref, wt_ref, wb_ref, *rest, emit):
    if emit:
        o_ref, wtb_ref, wbb_ref = rest
        wtb_ref[...] = wt_ref[...].astype(BF16)
        wbb_ref[...] = wb_ref[...].astype(BF16)
    else:
        (o_ref,), wtb_ref, wbb_ref = rest, wt_ref, wb_ref
    o_ref[...] = (x_ref[...]
                  + jnp.dot(og_ref[...], wtb_ref[...], preferred_element_type=F32)
                  + jnp.dot(oa_ref[...], wbb_ref[...], preferred_element_type=F32))


def _outproj(x, og, oa, w_top, w_bot, l, tm, emit, tn=512):
    m, d = x.shape
    half = og.shape[1]
    assert not emit or m == tm
    if emit:
        w_specs = [pl.BlockSpec((None, half, tn), lambda i, j: (l, 0, j)),
                   pl.BlockSpec((None, half, tn), lambda i, j: (l, 1, j))]
    else:
        w_specs = [pl.BlockSpec((half, tn), lambda i, j: (0, j))] * 2
    out_specs = [pl.BlockSpec((tm, tn), lambda i, j: (i, j))]
    out_shape = [jax.ShapeDtypeStruct((m, d), F32)]
    if emit:
        out_specs += [pl.BlockSpec((half, tn), lambda i, j: (0, j))] * 2
        out_shape += [jax.ShapeDtypeStruct((half, d), BF16)] * 2
    return pl.pallas_call(
        functools.partial(_outproj_kernel, emit=emit),
        grid=(m // tm, d // tn),
        in_specs=[
            pl.BlockSpec((tm, tn), lambda i, j: (i, j)),
            pl.BlockSpec((tm, half), lambda i, j: (i, 0)),
            pl.BlockSpec((tm, half), lambda i, j: (i, 0)),
        ] + w_specs,
        out_specs=out_specs,
        out_shape=out_shape,
        compiler_params=_params(("arbitrary", "arbitrary")),
        name="outproj",
    )(x, og, oa, w_top, w_bot)


def _mlp_kernel(x_ref, g_ref, wu_ref, wd_ref, *rest, na, tf, tn, emit):
    if emit:
        o_ref, wub_ref, wdb_ref, xn_ref, h_ref = rest
    else:
        o_ref, xn_ref, h_ref = rest
        wub_ref, wdb_ref = wu_ref, wd_ref
    j = pl.program_id(1)
    half = x_ref.shape[0] // 2
    parts = (slice(0, half), slice(half, None)) if emit else (slice(None),)

    @pl.when(j == 0)
    def _():
        xn_ref[...] = _rms(x_ref[...], g_ref[...]).astype(BF16)

    @pl.when(j < na)
    def _():
        if emit:
            wub_ref[...] = wu_ref[...].astype(BF16)
        u = jnp.concatenate([jnp.dot(xn_ref[rs, :], wub_ref[...], preferred_element_type=F32)
                             for rs in parts], axis=0)
        h_ref[j] = jnp.square(jnp.maximum(u, 0.0)).astype(BF16)

    @pl.when(j >= na)
    def _():
        if emit:
            wdb_ref[...] = wd_ref[...].astype(BF16)
        cols = pl.ds(pl.multiple_of((j - na) * tn, tn), tn)
        accs = [x_ref[rs, cols] for rs in parts]
        for c in range(na):
            w_c = wdb_ref[c * tf:(c + 1) * tf, :]
            accs = [acc + jnp.dot(h_ref[c, rs, :], w_c, preferred_element_type=F32)
                    for acc, rs in zip(accs, parts)]
        o_ref[...] = jnp.concatenate(accs, axis=0)


def _mlp(x, g, w_up, w_down, l, tm, tf, tn, emit):
    m, d = x.shape
    ff = w_up.shape[-1]
    na, nb = ff // tf, d // tn
    assert not emit or m == tm
    up_idx = lambda i, j: (0, jnp.minimum(j, na - 1))
    dn_idx = lambda i, j: (0, jnp.maximum(j - na, 0))
    if emit:
        w_specs = [pl.BlockSpec((None, d, tf), lambda i, j: (l,) + up_idx(i, j)),
                   pl.BlockSpec((None, ff, tn), lambda i, j: (l,) + dn_idx(i, j))]
    else:
        w_specs = [pl.BlockSpec((d, tf), up_idx), pl.BlockSpec((ff, tn), dn_idx)]
    out_specs = [pl.BlockSpec((tm, tn), lambda i, j: (i, jnp.maximum(j - na, 0)))]
    out_shape = [jax.ShapeDtypeStruct((m, d), F32)]
    if emit:
        out_specs += [pl.BlockSpec((d, tf), up_idx), pl.BlockSpec((ff, tn), dn_idx)]
        out_shape += [jax.ShapeDtypeStruct((d, ff), BF16), jax.ShapeDtypeStruct((ff, d), BF16)]
    return pl.pallas_call(
        functools.partial(_mlp_kernel, na=na, tf=tf, tn=tn, emit=emit),
        grid=(m // tm, na + nb),
        in_specs=[
            pl.BlockSpec((tm, d), lambda i, j: (i, 0), pipeline_mode=pl.Buffered(1)),
            pl.BlockSpec((None, 1, d), lambda i, j: (l, 0, 0)),
        ] + w_specs,
        out_specs=out_specs,
        out_shape=out_shape,
        scratch_shapes=[pltpu.VMEM((tm, d), BF16), pltpu.VMEM((na, tm, tf), BF16)],
        compiler_params=_params(("arbitrary", "arbitrary")),
        name="mlp",
    )(x, g, w_up, w_down)


def _ple_kernel(x_ref, g_ref, p_ref, wg_ref, wp_ref, gf_ref, *rest, tn, final, emit):
    if emit:
        o_ref, wgb_ref, xn_ref = rest
        wgb_ref[...] = wg_ref[...].astype(BF16)
    else:
        (o_ref, xn_ref), wgb_ref = rest, wg_ref
    j = pl.program_id(1)

    @pl.when(j == 0)
    def _():
        xn_ref[...] = _rms(x_ref[...], g_ref[...]).astype(BF16)

    cols = pl.ds(pl.multiple_of(j * tn, tn), tn)
    gate = jax.nn.sigmoid(jnp.dot(xn_ref[...], wgb_ref[...], preferred_element_type=F32))
    o_ref[:, cols] = x_ref[:, cols] + gate * _dot(p_ref[...], wp_ref[...])

    if final:
        @pl.when(j == pl.num_programs(1) - 1)
        def _():
            o_ref[...] = _rms(o_ref[...], gf_ref[...])


def _ple(x, g, p, w_gate, w_proj, g_final, l, tm, final, emit, tn=512):
    m, d = x.shape
    pd = p.shape[-1]
    assert not emit or m == tm
    out_specs = [pl.BlockSpec((tm, d), lambda i, j: (i, 0))]
    out_shape = [jax.ShapeDtypeStruct((m, d), F32)]
    if emit:
        wg_spec = pl.BlockSpec((None, d, tn), lambda i, j: (l, 0, j))
        out_specs += [pl.BlockSpec((d, tn), lambda i, j: (0, j))]
        out_shape += [jax.ShapeDtypeStruct((d, d), BF16)]
    else:
        wg_spec = pl.BlockSpec((d, tn), lambda i, j: (0, j))
    return pl.pallas_call(
        functools.partial(_ple_kernel, tn=tn, final=final, emit=emit),
        grid=(m // tm, d // tn),
        in_specs=[
            pl.BlockSpec((tm, d), lambda i, j: (i, 0)),
            pl.BlockSpec((None, 1, d), lambda i, j: (l, 0, 0)),
            pl.BlockSpec((None, tm, pd), lambda i, j: (l, i, 0)),
            wg_spec,
            pl.BlockSpec((None, pd, tn), lambda i, j: (l, 0, j)),
            pl.BlockSpec((1, d), lambda i, j: (0, 0)),
        ],
        out_specs=out_specs,
        out_shape=out_shape,
        scratch_shapes=[pltpu.VMEM((tm, d), BF16)],
        compiler_params=_params(("arbitrary", "arbitrary")),
        name="ple",
    )(x, g, p, w_gate, w_proj, g_final)


def kernel(x_prompt, x_sample, cache_k_win, cache_v_win, state_gla, p_prompt, p_sample,
           norm_mix, w_in, w_gate2, b_gate, gla_norm, w_out, norm_mlp, w_up, w_down,
           norm_ple, w_ple_gate, w_ple_proj, norm_final):
    bp, tp, d = x_prompt.shape
    bs, ts, _ = x_sample.shape
    depth = w_in.shape[0]
    mp, ms = bp * tp, bs * ts
    tm_p, tm_s = 1024, ms
    w_keep = min(MAX_WINDOW, tp)

    slopes_np = (2.0 ** (-8.0 * np.arange(1, ATT_HEADS + 1) / ATT_HEADS)).astype(np.float32)
    slopes = jnp.asarray(slopes_np)
    slope_rows = jnp.asarray(np.broadcast_to(np.tile(slopes_np, ts)[:, None], (ts * ATT_HEADS, LANES)))
    lr0 = GLA_PART
    w_z = jnp.swapaxes(w_in, 1, 2).astype(BF16)
    w_att = w_z[:, lr0 + GATE_RANK:]
    w_lr = jnp.pad(w_z[:, lr0:lr0 + GATE_RANK], ((0, 0), (0, LANES - GATE_RANK), (0, 0)))
    w2p = jnp.pad(w_gate2, ((0, 0), (0, LANES - GATE_RANK), (0, 0)))
    w_pp_b = w_ple_proj.astype(BF16)
    row = lambda a: a.reshape(depth, 1, a.shape[-1])
    nmix, nmlp, nple, bg, gn = row(norm_mix), row(norm_mlp), row(norm_ple), row(b_gate), row(gla_norm)
    gfin = norm_final.reshape(1, d)
    pp = p_prompt.reshape(depth, mp, -1)
    ps = p_sample.reshape(depth, ms, -1)

    hp = x_prompt.reshape(mp, d)
    hs = x_sample.reshape(ms, d)
    kp_l, vp_l, gp_l, ks_l, vs_l = [], [], [], [], []
    gs = None
    for l in range(depth):
        last = l == depth - 1
        z, glr = _inproj(hp, nmix, w_z, w_att, w_lr, l, tm_p)
        z3 = z.reshape(bp, tp, Z_COLS)
        og, s_new = _gla_prompt(z3, glr.reshape(bp, tp, LANES), w2p, bg, gn, l)
        zs, glr_s = _inproj(hs, nmix, w_z, w_att, w_lr, l, tm_s)
        og_s, gs = _gla_sample(zs, glr_s, w2p, bg, gn, state_gla, gs, l, bs, ts)
        heads = lambda c0: zs[:, c0:c0 + ATT_HEADS * ATT_DH].reshape(bs, ts, ATT_HEADS, ATT_DH)
        q_s, k_s, v_s = heads(ZC_QA), heads(ZC_KA), heads(ZC_VA)
        oa, oa_s = _attn(slopes, slope_rows, z3, q_s, k_s, v_s, cache_k_win, cache_v_win, l)
        kp_l.append(z3[:, tp - w_keep:, ZC_KA:ZC_VA].reshape(bp, w_keep, ATT_HEADS, ATT_DH))
        vp_l.append(z3[:, tp - w_keep:, ZC_VA:].reshape(bp, w_keep, ATT_HEADS, ATT_DH))
        gp_l.append(s_new)
        ks_l.append(k_s)
        vs_l.append(v_s)
        hs, wo_top, wo_bot = _outproj(hs, og_s, oa_s.reshape(ms, -1).astype(BF16), w_out, w_out,
                                      l, tm_s, emit=True)
        hs, w_up_b, w_down_b = _mlp(hs, nmlp, w_up, w_down, l, tm_s, tf=512, tn=256, emit=True)
        hs, w_pg_b = _ple(hs, nple, ps, w_ple_gate, w_pp_b, gfin, l, tm_s, last, emit=True)
        hp, = _outproj(hp, og.reshape(mp, -1), oa.reshape(mp, -1), wo_top, wo_bot, l, 2 * tm_p,
                       emit=False)
        hp, = _mlp(hp, nmlp, w_up_b, w_down_b, l, tm_p, tf=512, tn=512, emit=False)
        hp, = _ple(hp, nple, pp, w_pg_b, w_pp_b, gfin, l, tm_p, last, emit=False, tn=1024)
    y_prompt = hp.reshape(bp, tp, d)
    y_sample = hs.reshape(bs, ts, d)
    return (y_prompt, y_sample, jnp.stack(kp_l), jnp.stack(vp_l), jnp.stack(gp_l),
            jnp.stack(ks_l), jnp.stack(vs_l), gs)
```

```python
import functools

import jax
import jax.numpy as jnp
import numpy as np
from jax import lax
from jax.experimental import pallas as pl
from jax.experimental.pallas import tpu as pltpu

F32 = jnp.float32
BF16 = jnp.bfloat16

EPS = 1e-6
NEG_INF = -1e30
GLA_HEADS = 4
GLA_DK = 128
GLA_DV = 256
GLA_TAU = 16.0
GLA_CHUNK = 64
ATT_HEADS = 8
ATT_DH = 128
DILATED_BRANCHES = ((128, 1), (512, 4), (2048, 16))
MAX_WINDOW = 2048
GATE_RANK = 16

LANES = 128
VMEM_LIMIT = 56 * 1024 * 1024
MLP_VMEM_LIMIT = 62 * 1024 * 1024

Z_COLS = 6144
GLA_PART = 3072
ZC_QG, ZC_KG, ZC_VG, ZC_RG, ZC_QA, ZC_KA, ZC_VA = 0, 512, 1024, 2048, 3072, 4096, 5120


def _params(sem, vmem=VMEM_LIMIT):
    return pltpu.CompilerParams(dimension_semantics=sem, vmem_limit_bytes=vmem)


def _rms(x, g):
    return x * lax.rsqrt(jnp.mean(x * x, axis=-1, keepdims=True) + EPS) * g


def _dot(a, b):
    return jnp.dot(a.astype(BF16), b.astype(BF16), preferred_element_type=F32)


def _dot_nt(a, b):
    return lax.dot_general(a.astype(BF16), b.astype(BF16), (((1,), (1,)), ((), ())),
                           preferred_element_type=F32)


def _split2(x):
    hi = x.astype(BF16)
    lo = (x - hi.astype(F32)).astype(BF16)
    return hi, lo


def _dot_hi(a, b):
    ah, al = _split2(a)
    bh, bl = _split2(b)
    d = lambda u, v: jnp.dot(u, v, preferred_element_type=F32)
    return d(ah, bh) + d(ah, bl) + d(al, bh)


def _dot_exact(m, x):
    hi = x.astype(BF16)
    r1 = x - hi.astype(F32)
    mid = r1.astype(BF16)
    lo = (r1 - mid.astype(F32)).astype(BF16)
    d = lambda v: jnp.dot(m, v, preferred_element_type=F32)
    return d(hi) + d(mid) + d(lo)


def _log_sigmoid(x):
    return jnp.minimum(x, 0.0) - jnp.log1p(jnp.exp(-jnp.abs(x)))


def _silu(x):
    return x * jax.nn.sigmoid(x)


def _inproj_kernel(x_ref, g_ref, wa_ref, wb_ref, wl_ref, z_ref, glr_ref, xn_ref, *, nja):
    j = pl.program_id(1)

    @pl.when(j == 0)
    def _():
        xb = _rms(x_ref[...], g_ref[...]).astype(BF16)
        xn_ref[...] = xb
        glr_ref[...] = _dot_nt(xb, wl_ref[...])

    @pl.when(j < nja)
    def _():
        z_ref[...] = _dot_nt(xn_ref[...], wa_ref[...])

    @pl.when(j >= nja)
    def _():
        z_ref[...] = _dot_nt(xn_ref[...], wb_ref[...])


def _inproj(x, g, w_t, w_att, w_lr, l, tm, tn=1024):
    m, d = x.shape
    assert GLA_PART % tn == 0 and Z_COLS % tn == 0
    nja = GLA_PART // tn
    return pl.pallas_call(
        functools.partial(_inproj_kernel, nja=nja),
        grid=(m // tm, Z_COLS // tn),
        in_specs=[
            pl.BlockSpec((tm, d), lambda i, j: (i, 0)),
            pl.BlockSpec((None, 1, d), lambda i, j: (l, 0, 0)),
            pl.BlockSpec((None, tn, d), lambda i, j: (l, jnp.minimum(j, nja - 1), 0)),
            pl.BlockSpec((None, tn, d), lambda i, j: (l, jnp.maximum(j - nja, 0), 0)),
            pl.BlockSpec((None, LANES, d), lambda i, j: (l, 0, 0)),
        ],
        out_specs=[
            pl.BlockSpec((tm, tn), lambda i, j: (i, j)),
            pl.BlockSpec((tm, LANES), lambda i, j: (i, 0)),
        ],
        out_shape=[jax.ShapeDtypeStruct((m, Z_COLS), F32),
                   jax.ShapeDtypeStruct((m, LANES), F32)],
        scratch_shapes=[pltpu.VMEM((tm, d), BF16)],
        compiler_params=_params(("arbitrary", "arbitrary")),
        name="inproj",
    )(x, g, w_t, w_att, w_lr)


def _gla_prompt_kernel(q_ref, k_ref, v_ref, r_ref, glr_ref, w2_ref, bg_ref, gn_ref,
                       og_ref, sout_ref, s_ref, *, tb, chunk):
    step = pl.program_id(1)

    @pl.when(step == 0)
    def _():
        s_ref[...] = jnp.zeros_like(s_ref)

    c = chunk
    ri = lax.broadcasted_iota(jnp.int32, (c, c), 0)
    ci = lax.broadcasted_iota(jnp.int32, (c, c), 1)
    causal = ci <= ri
    csh = int(np.log2(c))
    ri2 = lax.broadcasted_iota(jnp.int32, (LANES, LANES), 0)
    ci2 = lax.broadcasted_iota(jnp.int32, (LANES, LANES), 1)
    same_chunk = (ri2 >> csh) == (ci2 >> csh)
    same2 = same_chunk.astype(BF16)
    tri2 = (same_chunk & (ci2 <= ri2)).astype(BF16)
    w2 = w2_ref[...]
    bg = bg_ref[...]
    gn = gn_ref[...]
    zpad_k = jnp.zeros((LANES - c, GLA_DK), F32)
    zpad_v = jnp.zeros((LANES - c, GLA_DV), BF16)

    g_all = _log_sigmoid(_dot_hi(glr_ref[...], w2) + bg) * (1.0 / GLA_TAU)
    b_parts, bl_parts = [], []
    for i in range(tb // LANES):
        g_pair = g_all[i * LANES:(i + 1) * LANES, :]
        b_parts.append(_dot_exact(tri2, g_pair))
        bl_parts.append(_dot_exact(same2, g_pair))
    b_all = jnp.concatenate(b_parts, axis=0)
    bl_all = jnp.concatenate(bl_parts, axis=0)
    e_q = jnp.exp(b_all)
    e_k = jnp.exp(-b_all)
    e_end = jnp.exp(bl_all - b_all)
    e_dec = jnp.exp(bl_all)
    state = [s_ref[h] for h in range(GLA_HEADS)]
    for n in range(tb // c):
        rows = slice(n * c, (n + 1) * c)
        for h in range(GLA_HEADS):
            ks = slice(h * GLA_DK, (h + 1) * GLA_DK)
            vs = slice(h * GLA_DV, (h + 1) * GLA_DV)
            q = q_ref[rows, ks] * (GLA_DK ** -0.5)
            k = k_ref[rows, ks]
            v = v_ref[rows, vs].astype(BF16)
            qd = (q * e_q[rows, ks]).astype(BF16)
            kd = k * e_k[rows, ks]
            ke = k * e_end[rows, ks]
            a = jnp.where(causal, _dot_nt(qd, kd), 0.0)
            s_old = state[h]
            o = _dot(a, v) + _dot(qd, s_old)
            ke_t = jnp.concatenate([ke, zpad_k], axis=0).T.astype(BF16)
            v_pad = jnp.concatenate([v, zpad_v], axis=0)
            ds = jnp.dot(ke_t, v_pad, preferred_element_type=F32)
            dec = jnp.broadcast_to(e_dec[n * c:n * c + 1, ks], (GLA_DK, GLA_DK)).T
            state[h] = jnp.concatenate([dec, dec], axis=1) * s_old + ds
            on = _rms(o, gn)
            og_ref[rows, vs] = (on * _silu(r_ref[rows, vs])).astype(og_ref.dtype)
    for h in range(GLA_HEADS):
        s_ref[h] = state[h]

    @pl.when(step == pl.num_programs(1) - 1)
    def _():
        sout_ref[...] = s_ref[...]


def _gla_prompt(z3, glr3, w2p, bg, gn, l, tb=512):
    b, t, _ = z3.shape
    chunk = np.gcd(t, GLA_CHUNK)
    assert t % tb == 0 and tb % LANES == 0 and LANES % chunk == 0 and chunk & (chunk - 1) == 0
    qk_w = GLA_HEADS * GLA_DK
    v_w = GLA_HEADS * GLA_DV
    return pl.pallas_call(
        functools.partial(_gla_prompt_kernel, tb=tb, chunk=int(chunk)),
        grid=(b, t // tb),
        in_specs=[
            pl.BlockSpec((None, tb, qk_w), lambda i, s: (i, s, ZC_QG // qk_w)),
            pl.BlockSpec((None, tb, qk_w), lambda i, s: (i, s, ZC_KG // qk_w)),
            pl.BlockSpec((None, tb, v_w), lambda i, s: (i, s, ZC_VG // v_w)),
            pl.BlockSpec((None, tb, v_w), lambda i, s: (i, s, ZC_RG // v_w)),
            pl.BlockSpec((None, tb, LANES), lambda i, s: (i, s, 0)),
            pl.BlockSpec((None, LANES, qk_w), lambda i, s: (l, 0, 0)),
            pl.BlockSpec((None, 1, qk_w), lambda i, s: (l, 0, 0)),
            pl.BlockSpec((None, 1, GLA_DV), lambda i, s: (l, 0, 0)),
        ],
        out_specs=[
            pl.BlockSpec((None, tb, v_w), lambda i, s: (i, s, 0)),
            pl.BlockSpec((None, GLA_HEADS, GLA_DK, GLA_DV), lambda i, s: (i, 0, 0, 0)),
        ],
        out_shape=[jax.ShapeDtypeStruct((b, t, v_w), BF16),
                   jax.ShapeDtypeStruct((b, GLA_HEADS, GLA_DK, GLA_DV), F32)],
        scratch_shapes=[pltpu.VMEM((GLA_HEADS, GLA_DK, GLA_DV), F32)],
        compiler_params=_params(("arbitrary", "arbitrary")),
        name="gla_prompt",
    )(z3, z3, z3, z3, glr3, w2p, bg, gn)


def _gla_sample_kernel(q_ref, k_ref, v_ref, r_ref, glr_ref, w2_ref, bg_ref, gn_ref, s0_ref,
                       *rest, nb, tq, layer):
    og_ref, sn_ref = rest[-2:]
    rws = nb * tq
    sh = int(np.log2(tq))
    ri = lax.broadcasted_iota(jnp.int32, (rws, rws), 0)
    ci = lax.broadcasted_iota(jnp.int32, (rws, rws), 1)
    same = (ri >> sh) == (ci >> sh)
    causal = same & (ci <= ri)
    x = _dot_hi(glr_ref[...], w2_ref[...]) + bg_ref[...]
    g = _log_sigmoid(x) * (1.0 / GLA_TAU)
    b = _dot_exact(causal.astype(BF16), g)
    bl = _dot_exact(same.astype(BF16), g)
    q = q_ref[...] * (GLA_DK ** -0.5)
    k = k_ref[...]
    qd = (q * jnp.exp(b)).astype(BF16)
    kd = k * jnp.exp(-b)
    ke = k * jnp.exp(bl - b)
    v = v_ref[...].astype(BF16)
    a = jnp.where(causal, _dot_nt(qd, kd), 0.0)
    o = _dot(a, v)
    ke_t = ke.T
    dec_t = jnp.exp(bl).T
    row_seq = lax.broadcasted_iota(jnp.int32, (rws, GLA_DV), 0) >> sh
    col_seq = lax.broadcasted_iota(jnp.int32, (GLA_DK, rws), 1) >> sh
    for s in range(nb):
        s0 = s0_ref[s]
        o = o + jnp.where(row_seq == s, _dot(qd, s0), 0.0)
        lhs = jnp.where(col_seq == s, ke_t, 0.0).astype(BF16)
        ds = jnp.dot(lhs, v, preferred_element_type=F32)
        s_new = dec_t[:, s * tq:s * tq + 1] * s0 + ds
        if layer is None:
            sn_ref[s] = s_new
        else:
            sn_ref[layer, s] = s_new
    if layer is not None:
        for other in range(sn_ref.shape[0]):
            if other != layer:
                sn_ref[other] = jnp.zeros(sn_ref.shape[1:], sn_ref.dtype)
    og_ref[...] = (_rms(o, gn_ref[...]) * _silu(r_ref[...])).astype(og_ref.dtype)


def _gla_sample(z, glr, w2p, bg, gn, state, stacked, l, nseq, tq, nb=32):
    assert nb * tq == LANES and nseq % nb == 0
    m = z.shape[0]
    depth = state.shape[0]
    prev = () if stacked is None else (stacked,)
    return pl.pallas_call(
        functools.partial(_gla_sample_kernel, nb=nb, tq=tq, layer=None if prev else l),
        grid=(nseq // nb, GLA_HEADS),
        in_specs=[
            pl.BlockSpec((LANES, GLA_DK), lambda i, h: (i, ZC_QG // GLA_DK + h)),
            pl.BlockSpec((LANES, GLA_DK), lambda i, h: (i, ZC_KG // GLA_DK + h)),
            pl.BlockSpec((LANES, GLA_DV), lambda i, h: (i, ZC_VG // GLA_DV + h)),
            pl.BlockSpec((LANES, GLA_DV), lambda i, h: (i, ZC_RG // GLA_DV + h)),
            pl.BlockSpec((LANES, LANES), lambda i, h: (i, 0)),
            pl.BlockSpec((None, LANES, GLA_DK), lambda i, h: (l, 0, h)),
            pl.BlockSpec((None, 1, GLA_DK), lambda i, h: (l, 0, h)),
            pl.BlockSpec((None, 1, GLA_DV), lambda i, h: (l, 0, 0)),
            pl.BlockSpec((None, nb, None, GLA_DK, GLA_DV), lambda i, h: (l, i, h, 0, 0)),
        ] + [pl.BlockSpec(memory_space=pl.ANY)] * len(prev),
        out_specs=[
            pl.BlockSpec((LANES, GLA_DV), lambda i, h: (i, h)),
            (pl.BlockSpec((None, nb, None, GLA_DK, GLA_DV), lambda i, h: (l, i, h, 0, 0)) if prev else
             pl.BlockSpec((depth, nb, None, GLA_DK, GLA_DV), lambda i, h: (0, i, h, 0, 0))),
        ],
        out_shape=[jax.ShapeDtypeStruct((m, GLA_HEADS * GLA_DV), BF16),
                   jax.ShapeDtypeStruct((depth, nseq, GLA_HEADS, GLA_DK, GLA_DV), F32)],
        input_output_aliases={9: 1} if prev else {},
        compiler_params=_params(("arbitrary", "arbitrary")),
        name="gla_sample",
    )(z, z, z, z, glr, w2p, bg, gn, state, *prev)


def _prompt_attention_blocks(slope, q_ref, k_ref, v_ref, accs, ms, ls, sub, *, t, branches, per_sub):
    n = LANES
    scale = ATT_DH ** -0.5
    qi = lax.broadcasted_iota(jnp.int32, (n, 2 * n), 0)
    kc = lax.broadcasted_iota(jnp.int32, (n, 2 * n), 1)
    delta = n + qi - kc
    band = (delta >= 0) & (delta <= n)
    own = kc >= n
    ones = jnp.ones((2 * n, n), BF16)

    for bi, (w, d) in enumerate(branches):
        nblk = t // (n * d)
        assert w // d == n and t % (n * d) == 0 and nblk & (nblk - 1) == 0
        if nblk == 2 and per_sub % 2 == 0:
            qi2 = lax.broadcasted_iota(jnp.int32, (2 * n, 2 * n), 0)
            kc2 = lax.broadcasted_iota(jnp.int32, (2 * n, 2 * n), 1)
            delta2 = qi2 - kc2
            band2 = (delta2 >= 0) & (delta2 <= n)
            bias2 = delta2.astype(F32) * (slope * d)
            for u in range(per_sub // 2):
                rows = pl.ds(sub * (per_sub // 2) + u, 2 * n, stride=d)
                q = q_ref[rows, :].astype(BF16)
                s = jnp.where(band2, _dot_nt(q, k_ref[rows, :]) * scale - bias2, NEG_INF)
                m_b = jnp.max(s, axis=-1, keepdims=True)
                p = jnp.exp(s - m_b).astype(BF16)
                ol = jnp.dot(p, jnp.concatenate([v_ref[rows, :].astype(BF16), ones], axis=1),
                             preferred_element_type=F32)
                accs[bi][rows, :] = ol[:, :n]
                ms[bi][rows, :] = jnp.broadcast_to(m_b, (2 * n, n))
                ls[bi][rows, :] = ol[:, n:]
            continue
        bias = delta.astype(F32) * (slope * d)
        for u in range(per_sub):
            idx = sub * per_sub + u
            r = idx >> int(np.log2(nblk))
            i = idx & (nblk - 1)
            start = r + i * (n * d)
            pstart = r + jnp.maximum(i - 1, 0) * (n * d)
            if d == 1:
                rows, prows = pl.ds(start, n), pl.ds(pstart, n)
            else:
                rows, prows = pl.ds(start, n, stride=d), pl.ds(pstart, n, stride=d)
            q = q_ref[rows, :].astype(BF16)
            k2 = jnp.concatenate([k_ref[prows, :].astype(BF16), k_ref[rows, :].astype(BF16)], axis=0)
            v2 = jnp.concatenate([v_ref[prows, :].astype(BF16), v_ref[rows, :].astype(BF16)], axis=0)
            valid = band & (own | (i > 0))
            s = jnp.where(valid, _dot_nt(q, k2) * scale - bias, NEG_INF)
            m_b = jnp.max(s, axis=-1, keepdims=True)
            p = jnp.exp(s - m_b).astype(BF16)
            ol = jnp.dot(p, jnp.concatenate([v2, ones], axis=1),
                         preferred_element_type=F32)
            accs[bi][rows, :] = ol[:, :n]
            ms[bi][rows, :] = jnp.broadcast_to(m_b, (n, n))
            ls[bi][rows, :] = ol[:, n:]


def _prompt_attention_combine(o_ref, accs, ms, ls, *, t, rows_per_step=256):
    def fin(c, carry):
        rows = pl.ds(pl.multiple_of(c * rows_per_step, rows_per_step), rows_per_step)
        m_parts = [m[rows, :] for m in ms]
        m_all = functools.reduce(jnp.maximum, m_parts)
        num = jnp.zeros((rows_per_step, LANES), F32)
        den = jnp.zeros((rows_per_step, LANES), F32)
        for acc, m_b, l in zip(accs, m_parts, ls):
            a = jnp.exp(m_b - m_all)
            num = num + a * acc[rows, :]
            den = den + a * l[rows, :]
        o_ref[rows, :] = (num / den).astype(o_ref.dtype)
        return carry

    lax.fori_loop(0, t // rows_per_step, fin, 0)


def _sample_attention(sl_ref, q_ref, kn_ref, vn_ref, khi_ref, klo_ref, vhi_ref, vlo_ref,
                      o_ref, cnt_ref, bias_ref, first_step, *, tq, wbuf, hi0, branches):
    nh = ATT_HEADS
    hsh = int(np.log2(nh))
    rq = tq * nh
    cp = LANES
    scale = ATT_DH ** -0.5
    slope = sl_ref[:, :1]
    n_hi = khi_ref.shape[0] // cp
    g = max(d for _, d in branches)
    nu = klo_ref.shape[1]
    mper = cp // nu
    ush = int(np.log2(nu))
    n_lo = klo_ref.shape[0] // mper

    def iotas(width):
        row = lax.broadcasted_iota(jnp.int32, (rq, width), 0)
        lane = lax.broadcasted_iota(jnp.int32, (rq, width), 1)
        return row >> hsh, (row & (nh - 1)) == (lane & (nh - 1)), lane >> hsh

    def multiplicity(dist, head_ok):
        cnt = jnp.zeros(dist.shape, F32)
        for (w, d) in branches:
            cnt = cnt + ((dist & (d - 1)) == 0).astype(F32) * (dist <= w).astype(F32)
        return jnp.where((dist >= 0) & head_ok, cnt, 0.0)

    @pl.when(first_step)
    def _():
        t_row, head_ok, pos_in = iotas(cp * nh)
        for c in range(n_hi):
            dist = wbuf + t_row - (hi0 + c * cp + pos_in)
            cnt_ref[c] = multiplicity(dist, head_ok)
            bias_ref[c] = slope * dist.astype(F32)
        for c in range(n_lo):
            pos = g * (c * mper + (pos_in >> ush)) + (pos_in & (nu - 1))
            dist = wbuf + t_row - pos
            cnt_ref[n_hi + c] = multiplicity(dist, head_ok)
            bias_ref[n_hi + c] = slope * dist.astype(F32)

    q2 = q_ref[...].reshape(rq, ATT_DH).astype(BF16)

    def slab(ref, c, per):
        return ref[c * per:(c + 1) * per].reshape(cp * nh, ATT_DH)

    def score(kc, cnt, bias):
        return jnp.where(cnt > 0.0, _dot_nt(q2, kc) * scale - bias, NEG_INF)

    keys = [slab(khi_ref, c, cp) for c in range(n_hi)] + [slab(klo_ref, c, mper) for c in range(n_lo)]
    vals = [slab(vhi_ref, c, cp) for c in range(n_hi)] + [slab(vlo_ref, c, mper) for c in range(n_lo)]
    scores = [score(kc, cnt_ref[c], bias_ref[c]) for c, kc in enumerate(keys)]
    t_row, head_ok, pos_in = iotas(rq)
    dist = t_row - pos_in
    cnt_new = multiplicity(dist, head_ok)
    s_new = score(kn_ref[...].reshape(rq, ATT_DH), cnt_new, slope * dist.astype(F32))
    m_all = scores[0]
    for s in scores[1:]:
        m_all = jnp.maximum(m_all, s)
    m = jnp.maximum(jnp.max(m_all, axis=-1, keepdims=True), jnp.max(s_new, axis=-1, keepdims=True))
    p_new = cnt_new * jnp.exp(s_new - m)
    acc = _dot(p_new, vn_ref[...].reshape(rq, ATT_DH))
    psum = jnp.zeros((rq, cp * nh), F32)
    for c, (s, vc) in enumerate(zip(scores, vals)):
        p = cnt_ref[c] * jnp.exp(s - m)
        psum = psum + p
        acc = acc + _dot(p, vc)
    l = jnp.sum(psum, axis=-1, keepdims=True) + jnp.sum(p_new, axis=-1, keepdims=True)
    o_ref[...] = (acc / l).reshape(tq, nh, ATT_DH)


def _attn_kernel(slopes_ref, q_ref, k_ref, v_ref, sl_ref, qs_ref, kn_ref, vn_ref,
                 khi_ref, klo_ref, vhi_ref, vlo_ref, o_ref, os_ref, *scratch,
                 t, branches, nsub, tq, wbuf, hi0):
    accs, ms, ls = scratch[0:3], scratch[3:6], scratch[6:9]
    cnt_ref, bias_ref = scratch[9:11]
    sub = pl.program_id(2)
    first_step = (pl.program_id(0) == 0) & (pl.program_id(1) == 0) & (sub == 0)
    _sample_attention(sl_ref, qs_ref, kn_ref, vn_ref, khi_ref, klo_ref, vhi_ref, vlo_ref,
                      os_ref, cnt_ref, bias_ref, first_step,
                      tq=tq, wbuf=wbuf, hi0=hi0, branches=branches)
    _prompt_attention_blocks(slopes_ref[pl.program_id(1)], q_ref, k_ref, v_ref, accs, ms, ls, sub,
                             t=t, branches=branches, per_sub=t // LANES // nsub)

    @pl.when(sub == nsub - 1)
    def _():
        _prompt_attention_combine(o_ref, accs, ms, ls, t=t)


def _attn(slopes, slope_rows, z3, q, kn, vn, cache_k, cache_v, l):
    b, t, _ = z3.shape
    nseq, tq, nh, dh = q.shape
    depth, _, wbuf, _, _ = cache_k.shape
    nsub = nseq // (b * nh)
    assert nsub * b * nh == nseq and (t // LANES) % nsub == 0
    g = max(d for _, d in DILATED_BRANCHES)
    w_dense = max(w for w, d in DILATED_BRANCHES if d < g)
    assert nh == ATT_HEADS and dh == ATT_DH and tq <= g and LANES % tq == 0
    assert all(d & (d - 1) == 0 for _, d in DILATED_BRANCHES)
    assert wbuf % w_dense == 0 and wbuf % g == 0 and w_dense % LANES == 0
    assert wbuf >= max(w for w, _ in DILATED_BRANCHES)
    hi0 = wbuf - w_dense
    n_lo = hi0 // g
    assert (n_lo * tq) % LANES == 0
    n_chunks = w_dense // LANES + n_lo * tq // LANES
    ck = cache_k.reshape(depth, nseq, wbuf // g, g, nh, dh)
    cv = cache_v.reshape(depth, nseq, wbuf // g, g, nh, dh)
    seq = lambda i, h, s: (i * nh + h) * nsub + s
    prm = lambda c0: pl.BlockSpec((None, t, dh), lambda i, h, s, sl: (i, 0, c0 // dh + h))
    new = pl.BlockSpec((None, tq, nh, dh), lambda i, h, s, sl: (seq(i, h, s), 0, 0, 0))
    hi = pl.BlockSpec((None, None, w_dense, nh, dh),
                      lambda i, h, s, sl: (l, seq(i, h, s), wbuf // w_dense - 1, 0, 0))
    lo = pl.BlockSpec((None, None, n_lo, tq, nh, dh), lambda i, h, s, sl: (l, seq(i, h, s), 0, 0, 0, 0))
    return pl.pallas_call(
        functools.partial(_attn_kernel, t=t, branches=DILATED_BRANCHES, nsub=nsub,
                          tq=tq, wbuf=wbuf, hi0=hi0),
        grid_spec=pltpu.PrefetchScalarGridSpec(
            num_scalar_prefetch=1,
            grid=(b, nh, nsub),
            in_specs=[prm(ZC_QA), prm(ZC_KA), prm(ZC_VA),
                      pl.BlockSpec((tq * nh, LANES), lambda i, h, s, sl: (0, 0)),
                      new, new, new, hi, lo, hi, lo],
            out_specs=[pl.BlockSpec((None, t, dh), lambda i, h, s, sl: (i, 0, h)), new],
            scratch_shapes=([pltpu.VMEM((t, dh), F32)] * 9
                            + [pltpu.VMEM((n_chunks, tq * nh, LANES * nh), F32)] * 2),
        ),
        out_shape=[jax.ShapeDtypeStruct((b, t, nh * dh), BF16),
                   jax.ShapeDtypeStruct((nseq, tq, nh, dh), F32)],
        compiler_params=_params(("arbitrary", "arbitrary", "arbitrary")),
        name="attn",
    )(slopes, z3, z3, z3, slope_rows, q, kn, vn, cache_k, ck, cache_v, cv)


def _outproj_kernel(x_ref, og_ref, oa_ref, wt_ref, wb_ref, *rest, emit):
    if emit:
        o_ref, wtb_ref, wbb_ref = rest
        wtb_ref[...] = wt_ref[...].astype(BF16)
        wbb_ref[...] = wb_ref[...].astype(BF16)
    else:
        (o_ref,), wtb_ref, wbb_ref = rest, wt_ref, wb_ref
    o_ref[...] = (x_ref[...]
                  + jnp.dot(og_ref[...], wtb_ref[...], preferred_element_type=F32)
                  + jnp.dot(oa_ref[...], wbb_ref[...], preferred_element_type=F32))


def _outproj(x, og, oa, w_top, w_bot, l, tm, emit, tn=512):
    m, d = x.shape
    half = og.shape[1]
    assert not emit or m == tm
    if emit:
        w_specs = [pl.BlockSpec((None, half, tn), lambda i, j: (l, 0, j)),
                   pl.BlockSpec((None, half, tn), lambda i, j: (l, 1, j))]
    else:
        w_specs = [pl.BlockSpec((half, tn), lambda i, j: (0, j))] * 2
    out_specs = [pl.BlockSpec((tm, tn), lambda i, j: (i, j))]
    out_shape = [jax.ShapeDtypeStruct((m, d), F32)]
    if emit:
        out_specs += [pl.BlockSpec((half, tn), lambda i, j: (0, j))] * 2
        out_shape += [jax.ShapeDtypeStruct((half, d), BF16)] * 2
    return pl.pallas_call(
        functools.partial(_outproj_kernel, emit=emit),
        grid=(m // tm, d // tn),
        in_specs=[
            pl.BlockSpec((tm, tn), lambda i, j: (i, j)),
            pl.BlockSpec((tm, half), lambda i, j: (i, 0)),
            pl.BlockSpec((tm, half), lambda i, j: (i, 0)),
        ] + w_specs,
        out_specs=out_specs,
        out_shape=out_shape,
        compiler_params=_params(("arbitrary", "arbitrary")),
        name="outproj",
    )(x, og, oa, w_top, w_bot)


def _mlp_kernel(x_ref, g_ref, wu_ref, wd_ref, *rest, na, tf, tn, emit):
    if emit:
        o_ref, wub_ref, wdb_ref, xn_ref, h_ref = rest
    else:
        o_ref, xn_ref, h_ref = rest
        wub_ref, wdb_ref = wu_ref, wd_ref
    j = pl.program_id(1)
    half = x_ref.shape[0] // 2
    parts = (slice(0, half), slice(half, None)) if emit else (slice(None),)

    @pl.when(j == 0)
    def _():
        xn_ref[...] = _rms(x_ref[...], g_ref[...]).astype(BF16)

    @pl.when(j < na)
    def _():
        if emit:
            wub_ref[...] = wu_ref[...].astype(BF16)
        u = jnp.concatenate([jnp.dot(xn_ref[rs, :], wub_ref[...], preferred_element_type=F32)
                             for rs in parts], axis=0)
        h_ref[j] = jnp.square(jnp.maximum(u, 0.0)).astype(BF16)

    @pl.when(j >= na)
    def _():
        if emit:
            wdb_ref[...] = wd_ref[...].astype(BF16)
        cols = pl.ds(pl.multiple_of((j - na) * tn, tn), tn)
        accs = [x_ref[rs, cols] for rs in parts]
        for c in range(na):
            w_c = wdb_ref[c * tf:(c + 1) * tf, :]
            accs = [acc + jnp.dot(h_ref[c, rs, :], w_c, preferred_element_type=F32)
                    for acc, rs in zip(accs, parts)]
        o_ref[...] = jnp.concatenate(accs, axis=0)


def _mlp(x, g, w_up, w_down, l, tm, tf, tn, emit):
    m, d = x.shape
    ff = w_up.shape[-1]
    na, nb = ff // tf, d // tn
    assert not emit or m == tm
    up_idx = lambda i, j: (0, jnp.minimum(j, na - 1))
    dn_idx = lambda i, j: (0, jnp.maximum(j - na, 0))
    if emit:
        w_specs = [pl.BlockSpec((None, d, tf), lambda i, j: (l,) + up_idx(i, j)),
                   pl.BlockSpec((None, ff, tn), lambda i, j: (l,) + dn_idx(i, j))]
    else:
        w_specs = [pl.BlockSpec((d, tf), up_idx), pl.BlockSpec((ff, tn), dn_idx)]
    out_specs = [pl.BlockSpec((tm, tn), lambda i, j: (i, jnp.maximum(j - na, 0)))]
    out_shape = [jax.ShapeDtypeStruct((m, d), F32)]
    if emit:
        out_specs += [pl.BlockSpec((d, tf), up_idx), pl.BlockSpec((ff, tn), dn_idx)]
        out_shape += [jax.ShapeDtypeStruct((d, ff), BF16), jax.ShapeDtypeStruct((ff, d), BF16)]
    return pl.pallas_call(
        functools.partial(_mlp_kernel, na=na, tf=tf, tn=tn, emit=emit),
        grid=(m // tm, na + nb),
        in_specs=[
            pl.BlockSpec((tm, d), lambda i, j: (i, 0),
                         pipeline_mode=pl.Buffered(1 if emit else 2)),
            pl.BlockSpec((None, 1, d), lambda i, j: (l, 0, 0)),
        ] + w_specs,
        out_specs=out_specs,
        out_shape=out_shape,
        scratch_shapes=[pltpu.VMEM((tm, d), BF16), pltpu.VMEM((na, tm, tf), BF16)],
        compiler_params=_params(("arbitrary", "arbitrary"), VMEM_LIMIT if emit else MLP_VMEM_LIMIT),
        name="mlp",
    )(x, g, w_up, w_down)


def _ple_kernel(x_ref, g_ref, p_ref, wg_ref, wp_ref, gf_ref, *rest, tn, final, emit):
    if emit:
        o_ref, wgb_ref, xn_ref = rest
        wgb_ref[...] = wg_ref[...].astype(BF16)
    else:
        (o_ref, xn_ref), wgb_ref = rest, wg_ref
    j = pl.program_id(1)

    @pl.when(j == 0)
    def _():
        xn_ref[...] = _rms(x_ref[...], g_ref[...]).astype(BF16)

    cols = pl.ds(pl.multiple_of(j * tn, tn), tn)
    gate = jax.nn.sigmoid(jnp.dot(xn_ref[...], wgb_ref[...], preferred_element_type=F32))
    o_ref[:, cols] = x_ref[:, cols] + gate * _dot(p_ref[...], wp_ref[...])

    if final:
        @pl.when(j == pl.num_programs(1) - 1)
        def _():
            o_ref[...] = _rms(o_ref[...], gf_ref[...])


def _ple(x, g, p, w_gate, w_proj, g_final, l, tm, final, emit, tn=512):
    m, d = x.shape
    pd = p.shape[-1]
    assert not emit or m == tm
    out_specs = [pl.BlockSpec((tm, d), lambda i, j: (i, 0))]
    out_shape = [jax.ShapeDtypeStruct((m, d), F32)]
    if emit:
        wg_spec = pl.BlockSpec((None, d, tn), lambda i, j: (l, 0, j))
        out_specs += [pl.BlockSpec((d, tn), lambda i, j: (0, j))]
        out_shape += [jax.ShapeDtypeStruct((d, d), BF16)]
    else:
        wg_spec = pl.BlockSpec((d, tn), lambda i, j: (0, j))
    return pl.pallas_call(
        functools.partial(_ple_kernel, tn=tn, final=final, emit=emit),
        grid=(m // tm, d // tn),
        in_specs=[
            pl.BlockSpec((tm, d), lambda i, j: (i, 0)),
            pl.BlockSpec((None, 1, d), lambda i, j: (l, 0, 0)),
            pl.BlockSpec((None, tm, pd), lambda i, j: (l, i, 0)),
            wg_spec,
            pl.BlockSpec((None, pd, tn), lambda i, j: (l, 0, j)),
            pl.BlockSpec((1, d), lambda i, j: (0, 0)),
        ],
        out_specs=out_specs,
        out_shape=out_shape,
        scratch_shapes=[pltpu.VMEM((tm, d), BF16)],
        compiler_params=_params(("arbitrary", "arbitrary")),
        name="ple",
    )(x, g, p, w_gate, w_proj, g_final)


def kernel(x_prompt, x_sample, cache_k_win, cache_v_win, state_gla, p_prompt, p_sample,
           norm_mix, w_in, w_gate2, b_gate, gla_norm, w_out, norm_mlp, w_up, w_down,
           norm_ple, w_ple_gate, w_ple_proj, norm_final):
    bp, tp, d = x_prompt.shape
    bs, ts, _ = x_sample.shape
    depth = w_in.shape[0]
    mp, ms = bp * tp, bs * ts
    tm_p, tm_s = 1024, ms
    w_keep = min(MAX_WINDOW, tp)

    slopes_np = (2.0 ** (-8.0 * np.arange(1, ATT_HEADS + 1) / ATT_HEADS)).astype(np.float32)
    slopes = jnp.asarray(slopes_np)
    slope_rows = jnp.asarray(np.broadcast_to(np.tile(slopes_np, ts)[:, None], (ts * ATT_HEADS, LANES)))
    lr0 = GLA_PART
    w_z = jnp.swapaxes(w_in, 1, 2).astype(BF16)
    w_att = w_z[:, lr0 + GATE_RANK:]
    w_lr = jnp.pad(w_z[:, lr0:lr0 + GATE_RANK], ((0, 0), (0, LANES - GATE_RANK), (0, 0)))
    w2p = jnp.pad(w_gate2, ((0, 0), (0, LANES - GATE_RANK), (0, 0)))
    w_pp_b = w_ple_proj.astype(BF16)
    row = lambda a: a.reshape(depth, 1, a.shape[-1])
    nmix, nmlp, nple, bg, gn = row(norm_mix), row(norm_mlp), row(norm_ple), row(b_gate), row(gla_norm)
    gfin = norm_final.reshape(1, d)
    pp = p_prompt.reshape(depth, mp, -1)
    ps = p_sample.reshape(depth, ms, -1)

    hp = x_prompt.reshape(mp, d)
    hs = x_sample.reshape(ms, d)
    kp_l, vp_l, gp_l, ks_l, vs_l = [], [], [], [], []
    gs = None
    for l in range(depth):
        last = l == depth - 1
        z, glr = _inproj(hp, nmix, w_z, w_att, w_lr, l, tm_p)
        z3 = z.reshape(bp, tp, Z_COLS)
        og, s_new = _gla_prompt(z3, glr.reshape(bp, tp, LANES), w2p, bg, gn, l)
        zs, glr_s = _inproj(hs, nmix, w_z, w_att, w_lr, l, tm_s)
        og_s, gs = _gla_sample(zs, glr_s, w2p, bg, gn, state_gla, gs, l, bs, ts)
        heads = lambda c0: zs[:, c0:c0 + ATT_HEADS * ATT_DH].reshape(bs, ts, ATT_HEADS, ATT_DH)
        q_s, k_s, v_s = heads(ZC_QA), heads(ZC_KA), heads(ZC_VA)
        oa, oa_s = _attn(slopes, slope_rows, z3, q_s, k_s, v_s, cache_k_win, cache_v_win, l)
        kp_l.append(z3[:, tp - w_keep:, ZC_KA:ZC_VA].reshape(bp, w_keep, ATT_HEADS, ATT_DH))
        vp_l.append(z3[:, tp - w_keep:, ZC_VA:].reshape(bp, w_keep, ATT_HEADS, ATT_DH))
        gp_l.append(s_new)
        ks_l.append(k_s)
        vs_l.append(v_s)
        hs, wo_top, wo_bot = _outproj(hs, og_s, oa_s.reshape(ms, -1).astype(BF16), w_out, w_out,
                                      l, tm_s, emit=True)
        hs, w_up_b, w_down_b = _mlp(hs, nmlp, w_up, w_down, l, tm_s, tf=512, tn=256, emit=True)
        hs, w_pg_b = _ple(hs, nple, ps, w_ple_gate, w_pp_b, gfin, l, tm_s, last, emit=True)
        hp, = _outproj(hp, og.reshape(mp, -1), oa.reshape(mp, -1), wo_top, wo_bot, l, 2 * tm_p,
                       emit=False)
        hp, = _mlp(hp, nmlp, w_up_b, w_down_b, l, tm_p, tf=512, tn=512, emit=False)
        hp, = _ple(hp, nple, pp, w_pg_b, w_pp_b, gfin, l, tm_p, last, emit=False, tn=1024)
    y_prompt = hp.reshape(bp, tp, d)
    y_sample = hs.reshape(bs, ts, d)
    return (y_prompt, y_sample, jnp.stack(kp_l), jnp.stack(vp_l), jnp.stack(gp_l),
            jnp.stack(ks_l), jnp.stack(vs_l), gs)
```
